```python
import jax, jax.numpy as jnp
from jax import lax
import numpy as np

D_MODEL = 1024
BATCH = 8
SEQ = 4096
DEPTH = 2

N_MEM = 256
XA_HEADS = 4
XA_HD = D_MODEL // XA_HEADS
W_A = D_MODEL
H_A = 8
HD_A = W_A // H_A
CONV_A = 4
C_RG = 8.0
W_B = D_MODEL // 2
POOL_WINDOWS = (2, 4, 8, 16)
G_B = len(POOL_WINDOWS)
HD_B = W_B // G_B
IN_AB = 2 * W_A + W_B
OUT_AB = W_A + W_B
CONV_C = 31
D_FF = 3 * D_MODEL
CONV_F = 3
EPS = 1e-6
N_EVEN = (DEPTH + 1) // 2
N_ODD = DEPTH // 2

kernel_name = "hybrid_rglru_pool_conformer_xattn_convffn"


def rms_norm(x, g):
    xf = x.astype(jnp.float32)
    y = xf * lax.rsqrt(jnp.mean(xf * xf, axis=-1, keepdims=True) + EPS)
    return (y * g.astype(jnp.float32)).astype(x.dtype)


def layer_norm(x, g, b):
    xf = x.astype(jnp.float32)
    mu = jnp.mean(xf, axis=-1, keepdims=True)
    var = jnp.mean(jnp.square(xf - mu), axis=-1, keepdims=True)
    y = (xf - mu) * lax.rsqrt(var + EPS)
    return (y * g.astype(jnp.float32) + b.astype(jnp.float32)).astype(x.dtype)


def causal_dwconv(x, w, b):
    K, C = w.shape
    y = lax.conv_general_dilated(
        x, w[:, None, :], window_strides=(1,), padding=[(K - 1, 0)],
        dimension_numbers=("NWC", "WIO", "NWC"), feature_group_count=C)
    return y + b


def rg_lru(x, w_gx, b_gx, w_ga, b_ga, lam):
    Bn, S, W = x.shape
    xh = x.reshape(Bn, S, H_A, HD_A)
    gate_x = jax.nn.sigmoid(jnp.einsum('bshi,hij->bshj', xh, w_gx).reshape(Bn, S, W) + b_gx)
    gate_a = jax.nn.sigmoid(jnp.einsum('bshi,hij->bshj', xh, w_ga).reshape(Bn, S, W) + b_ga)
    log_a = -C_RG * gate_a.astype(jnp.float32) * jax.nn.softplus(-lam.astype(jnp.float32))
    a = jnp.exp(log_a)
    mult = jnp.sqrt(-jnp.expm1(2.0 * log_a))
    bx = mult * (gate_x * x).astype(jnp.float32)

    def combine(lhs, rhs):
        a_l, b_l = lhs
        a_r, b_r = rhs
        return a_l * a_r, a_r * b_l + b_r

    _, h = lax.associative_scan(combine, (a, bx), axis=1)
    return h.astype(x.dtype)


def multi_scale_pool(u, w_g, b_g, scale):
    Bn, S, W = u.shape
    uf = u.astype(jnp.float32)
    csum = jnp.cumsum(uf, axis=1)
    t = jnp.arange(1, S + 1, dtype=jnp.float32)[:, None]
    outs = []
    for g, w in enumerate(POOL_WINDOWS):
        sl = slice(g * HD_B, (g + 1) * HD_B)
        cg = csum[..., sl]
        lagged = jnp.pad(cg[:, :S - w], ((0, 0), (w, 0), (0, 0)))
        mean = (cg - lagged) / jnp.minimum(t, float(w))
        outs.append(mean - uf[..., sl])
    p = jnp.stack(outs, axis=2).astype(u.dtype)
    y = jnp.einsum('bsgi,gij->bsgj', p, w_g).reshape(Bn, S, W) + b_g
    return y * scale


def mixer_ab(x, norm, w_in, conv_w, conv_b, w_gx, b_gx, w_ga, b_ga, lam, w_pool, b_pool, pool_scale, w_out):
    z = rms_norm(x, norm) @ w_in
    z_gate = z[..., :W_A]
    z_rec = z[..., W_A:2 * W_A]
    z_pool = z[..., 2 * W_A:]
    xr = causal_dwconv(z_rec, conv_w, conv_b)
    y_a = jax.nn.gelu(z_gate) * rg_lru(xr, w_gx, b_gx, w_ga, b_ga, lam)
    y_b = multi_scale_pool(z_pool, w_pool, b_pool, pool_scale)
    return jnp.concatenate([y_a, y_b], axis=-1) @ w_out


def conformer_conv(x, norm, w1, b1, dw_w, dw_b, ln_g, ln_b, w2, b2):
    h = rms_norm(x, norm) @ w1 + b1
    h = jax.nn.glu(h, axis=-1)
    h = causal_dwconv(h, dw_w, dw_b)
    h = jax.nn.silu(layer_norm(h, ln_g, ln_b))
    return h @ w2 + b2


def cross_attn(x, mem, norm, mem_norm, wq, wk, wv, wo):
    Bn, S, _ = x.shape
    M = mem.shape[1]
    q = (rms_norm(x, norm) @ wq).reshape(Bn, S, XA_HEADS, XA_HD)
    m = rms_norm(mem, mem_norm)
    k = (m @ wk).reshape(Bn, M, XA_HEADS, XA_HD)
    v = (m @ wv).reshape(Bn, M, XA_HEADS, XA_HD)
    s = jnp.einsum('bqhd,bkhd->bhqk', q, k).astype(jnp.float32) * (XA_HD ** -0.5)
    p = jax.nn.softmax(s, axis=-1).astype(x.dtype)
    o = jnp.einsum('bhqk,bkhd->bqhd', p, v).reshape(Bn, S, D_MODEL)
    return o @ wo


def conv_ffn(x, norm, w_up, dw_w, dw_b, w_down):
    h = rms_norm(x, norm) @ w_up
    g = causal_dwconv(h[..., :D_FF], dw_w, dw_b)
    u = h[..., D_FF:]
    return (jax.nn.gelu(g) * u) @ w_down


def setup_inputs(seed: int = 0) -> dict:
    key = jax.random.key(seed)
    keys = iter(jax.random.split(key, 48))

    def nrm(shape, scale):
        return jax.random.normal(next(keys), shape, jnp.float32) * scale

    def gain(shape):
        return 1.0 + 0.02 * jax.random.normal(next(keys), shape, jnp.float32)

    L, NE, NO, D = DEPTH, N_EVEN, N_ODD, D_MODEL
    u = jax.random.uniform(next(keys), (NE, W_A), jnp.float32, 0.9, 0.999) ** (1.0 / C_RG)
    lam = jnp.log(u) - jnp.log1p(-u)
    return {
        "x": jax.random.normal(next(keys), (BATCH, SEQ, D), jnp.float32),
        "mem": jax.random.normal(next(keys), (BATCH, N_MEM, D), jnp.float32),
        "ab_norm": gain((NE, D)),
        "ab_w_in": nrm((NE, D, IN_AB), D ** -0.5),
        "a_conv_w": nrm((NE, CONV_A, W_A), CONV_A ** -0.5),
        "a_conv_b": nrm((NE, W_A), 0.01),
        "a_gate_x_w": nrm((NE, H_A, HD_A, HD_A), HD_A ** -0.5),
        "a_gate_x_b": nrm((NE, W_A), 0.01),
        "a_gate_a_w": nrm((NE, H_A, HD_A, HD_A), HD_A ** -0.5),
        "a_gate_a_b": nrm((NE, W_A), 0.01),
        "a_lambda": lam,
        "b_group_w": nrm((NE, G_B, HD_B, HD_B), HD_B ** -0.5),
        "b_group_b": nrm((NE, W_B), 0.01),
        "b_scale": 1.0 + 0.1 * jax.random.normal(next(keys), (NE, W_B), jnp.float32),
        "ab_w_out": nrm((NE, OUT_AB, D), OUT_AB ** -0.5),
        "c_norm": gain((NO, D)),
        "c_w_pw1": nrm((NO, D, 2 * D), D ** -0.5),
        "c_b_pw1": nrm((NO, 2 * D), 0.01),
        "c_dw_w": nrm((NO, CONV_C, D), CONV_C ** -0.5),
        "c_dw_b": nrm((NO, D), 0.01),
        "c_ln_g": gain((NO, D)),
        "c_ln_b": nrm((NO, D), 0.01),
        "c_w_pw2": nrm((NO, D, D), D ** -0.5),
        "c_b_pw2": nrm((NO, D), 0.01),
        "xa_norm": gain((L, D)),
        "xa_mem_norm": gain((L, D)),
        "xa_wq": nrm((L, D, D), D ** -0.5),
        "xa_wk": nrm((L, D, D), D ** -0.5),
        "xa_wv": nrm((L, D, D), D ** -0.5),
        "xa_wo": nrm((L, D, D), D ** -0.5),
        "f_norm": gain((L, D)),
        "f_w_up": nrm((L, D, 2 * D_FF), D ** -0.5),
        "f_dw_w": nrm((L, CONV_F, D_FF), CONV_F ** -0.5),
        "f_dw_b": nrm((L, D_FF), 0.01),
        "f_w_down": nrm((L, D_FF, D), D_FF ** -0.5),
        "final_norm": gain((D,)),
    }


def reference(x, mem, ab_norm, ab_w_in, a_conv_w, a_conv_b, a_gate_x_w, a_gate_x_b,
              a_gate_a_w, a_gate_a_b, a_lambda, b_group_w, b_group_b, b_scale, ab_w_out,
              c_norm, c_w_pw1, c_b_pw1, c_dw_w, c_dw_b, c_ln_g, c_ln_b, c_w_pw2, c_b_pw2,
              xa_norm, xa_mem_norm, xa_wq, xa_wk, xa_wv, xa_wo,
              f_norm, f_w_up, f_dw_w, f_dw_b, f_w_down, final_norm):
    for layer in range(DEPTH):
        if layer % 2 == 0:
            i = layer // 2
            x = x + mixer_ab(x, ab_norm[i], ab_w_in[i], a_conv_w[i], a_conv_b[i],
                             a_gate_x_w[i], a_gate_x_b[i], a_gate_a_w[i], a_gate_a_b[i],
                             a_lambda[i], b_group_w[i], b_group_b[i], b_scale[i], ab_w_out[i])
        else:
            j = layer // 2
            x = x + conformer_conv(x, c_norm[j], c_w_pw1[j], c_b_pw1[j], c_dw_w[j], c_dw_b[j],
                                   c_ln_g[j], c_ln_b[j], c_w_pw2[j], c_b_pw2[j])
        x = x + cross_attn(x, mem, xa_norm[layer], xa_mem_norm[layer], xa_wq[layer],
                           xa_wk[layer], xa_wv[layer], xa_wo[layer])
        x = x + conv_ffn(x, f_norm[layer], f_w_up[layer], f_dw_w[layer], f_dw_b[layer],
                         f_w_down[layer])
    return rms_norm(x, final_norm)
```

```python
import functools
import math

import jax
import jax.numpy as jnp
from jax import lax
from jax.experimental import pallas as pl
from jax.experimental.pallas import tpu as pltpu

EPS = 1e-6
C_RG = 8.0
POOL_WINDOWS = (2, 4, 8, 16)
SUBLANES = 8
LANES = 128
VMEM_LIMIT_BYTES = 56 * 1024 * 1024

_BF = jnp.bfloat16
_F32 = jnp.float32


def _dot(a, b):
    return jnp.dot(a, b, preferred_element_type=_F32)


def _rms(x, g):
    return x * lax.rsqrt(jnp.mean(x * x, axis=-1, keepdims=True) + EPS) * g


def _sigmoid(x):
    return 1.0 / (1.0 + jnp.exp(-x))


def _gelu_tanh(x):
    c = math.sqrt(2.0 / math.pi)
    return 0.5 * x * (1.0 + jnp.tanh(c * (x + 0.044715 * (x * x * x))))


def _softplus(x):
    return jnp.maximum(x, 0.0) + jnp.log1p(jnp.exp(-jnp.abs(x)))


def _const_spec(shape):
    nd = len(shape)
    return pl.BlockSpec(shape, lambda *_: (0,) * nd, pipeline_mode=pl.Buffered(1))


def _params(n_grid):
    return pltpu.CompilerParams(
        dimension_semantics=("arbitrary",) * n_grid,
        vmem_limit_bytes=VMEM_LIMIT_BYTES,
    )


def _ab_kernel(x_ref, g_ref, win_ref, cw_ref, cb_ref, wg_ref, bgx_ref, bga_ref, lam_ref,
               wp_ref, bp_ref, sc_ref, wout_ref, o_ref,
               zrec_buf, pool_buf, a_buf, bx_buf, h_buf, h_carry, *, tt, wa, wb, n_heads):
    i = pl.program_id(0)
    rows = tt * SUBLANES
    conv_k = cw_ref.shape[0]
    halo_c = (conv_k - 1) * SUBLANES
    halo_p = POOL_WINDOWS[-1] * SUBLANES
    hd = wa // n_heads
    hdb = wb // len(POOL_WINDOWS)

    @pl.when(i == 0)
    def _():
        zrec_buf[0:halo_c, :] = jnp.zeros((halo_c, wa), _F32)
        pool_buf[0:halo_p, :] = jnp.zeros((halo_p, wb), _F32)
        h_carry[...] = jnp.zeros_like(h_carry)

    x = x_ref[...]
    xn = _rms(x, g_ref[...]).astype(_BF)

    zrec_buf[halo_c:halo_c + rows, :] = _dot(xn, win_ref[:, wa:2 * wa])
    xr = cb_ref[...] + cw_ref[0:1, :] * zrec_buf[0:rows, :]
    for k in range(1, conv_k):
        xr = xr + cw_ref[k:k + 1, :] * zrec_buf[k * SUBLANES:k * SUBLANES + rows, :]
    zrec_buf[0:halo_c, :] = zrec_buf[rows:rows + halo_c, :]

    xr_b = xr.astype(_BF)
    sp = _softplus(-lam_ref[...])
    for h in range(n_heads):
        hs = slice(h * hd, (h + 1) * hd)
        gg = _dot(xr_b[:, hs], wg_ref[h])
        gate_x = _sigmoid(gg[:, :hd] + bgx_ref[:, hs])
        gate_a = _sigmoid(gg[:, hd:] + bga_ref[:, hs])
        a = jnp.exp((-C_RG) * gate_a * sp[:, hs])
        a_buf[:, hs] = a
        bx_buf[:, hs] = jnp.sqrt(1.0 - a * a) * (gate_x * xr[:, hs])

    h = h_carry[...]
    for t in range(tt):
        ts = slice(t * SUBLANES, (t + 1) * SUBLANES)
        h = a_buf[ts, :] * h + bx_buf[ts, :]
        h_buf[ts, :] = h
    h_carry[...] = h

    y_a = (_gelu_tanh(_dot(xn, win_ref[:, 0:wa])) * h_buf[...]).astype(_BF)

    z_pool = _dot(xn, win_ref[:, 2 * wa:2 * wa + wb])
    pool_buf[halo_p:halo_p + rows, :] = z_pool
    t_idx = i * tt + lax.broadcasted_iota(jnp.int32, (rows, 1), 0) // SUBLANES
    yb = []
    for g, w in enumerate(POOL_WINDOWS):
        gs = slice(g * hdb, (g + 1) * hdb)
        cur = pool_buf[:, gs]
        shift = 1
        while shift < w:
            cur = cur[shift * SUBLANES:, :] + cur[:cur.shape[0] - shift * SUBLANES, :]
            shift *= 2
        win_sum = cur[cur.shape[0] - rows:, :]
        cnt = jnp.minimum(t_idx + 1, w).astype(_F32)
        p = (win_sum / cnt - z_pool[:, gs]).astype(_BF)
        yb.append((_dot(p, wp_ref[g]) + bp_ref[:, gs]) * sc_ref[:, gs])
    pool_buf[0:halo_p, :] = pool_buf[rows:rows + halo_p, :]
    y_b = jnp.concatenate(yb, axis=1).astype(_BF)

    o_ref[...] = x + _dot(y_a, wout_ref[0:wa, :]) + _dot(y_b, wout_ref[wa:wa + wb, :])


def _mixer_ab(xt, norm, w_in, conv_w, conv_b, w_gx, b_gx, w_ga, b_ga, lam, w_pool, b_pool,
              pool_scale, w_out, *, tt):
    n_rows, d = xt.shape
    n_heads, hd, _ = w_gx.shape
    wa = n_heads * hd
    wb = w_pool.shape[0] * w_pool.shape[1]
    rows = tt * SUBLANES
    conv_k = conv_w.shape[0]
    halo_c = (conv_k - 1) * SUBLANES
    halo_p = POOL_WINDOWS[-1] * SUBLANES
    assert n_rows % rows == 0 and rows >= halo_p and w_in.shape == (d, 2 * wa + wb)
    wg = jnp.concatenate([w_gx, w_ga], axis=-1).astype(_BF)
    row = lambda v: v.reshape(1, -1)
    args = (xt, row(norm), w_in.astype(_BF), conv_w, row(conv_b), wg, row(b_gx), row(b_ga),
            row(lam), w_pool.astype(_BF), row(b_pool), row(pool_scale), w_out.astype(_BF))
    tile = pl.BlockSpec((rows, d), lambda i: (i, 0))
    return pl.pallas_call(
        functools.partial(_ab_kernel, tt=tt, wa=wa, wb=wb, n_heads=n_heads),
        grid=(n_rows // rows,),
        in_specs=[tile] + [_const_spec(a.shape) for a in args[1:]],
        out_specs=tile,
        out_shape=jax.ShapeDtypeStruct((n_rows, d), _F32),
        scratch_shapes=[
            pltpu.VMEM((halo_c + rows, wa), _F32),
            pltpu.VMEM((halo_p + rows, wb), _F32),
            pltpu.VMEM((rows, wa), _F32),
            pltpu.VMEM((rows, wa), _F32),
            pltpu.VMEM((rows, wa), _F32),
            pltpu.VMEM((SUBLANES, wa), _F32),
        ],
        compiler_params=_params(1),
        name="mixer_ab",
    )(*args)


def _conf_kernel(x_ref, g_ref, w1_ref, b1_ref, dw_ref, db_ref, lg_ref, lb_ref, w2_ref, b2_ref,
                 o_ref, glu_buf, conv_buf, *, tt, chunk):
    i = pl.program_id(0)
    rows = tt * SUBLANES
    d = x_ref.shape[1]
    conv_k = dw_ref.shape[0]
    halo = (conv_k - 1) * SUBLANES

    @pl.when(i == 0)
    def _():
        glu_buf[0:halo, :] = jnp.zeros((halo, d), _F32)

    x = x_ref[...]
    xn = _rms(x, g_ref[...]).astype(_BF)
    ha = _dot(xn, w1_ref[:, 0:d]) + b1_ref[:, 0:d]
    hb = _dot(xn, w1_ref[:, d:2 * d]) + b1_ref[:, d:2 * d]
    glu_buf[halo:halo + rows, :] = ha * _sigmoid(hb)

    def conv_chunk(c, carry):
        r0 = pl.multiple_of(c * chunk, chunk)
        for j in range(d // LANES):
            ls = slice(j * LANES, (j + 1) * LANES)
            acc = db_ref[:, ls] + dw_ref[0:1, ls] * glu_buf[pl.ds(r0, chunk), ls]
            for k in range(1, conv_k):
                acc = acc + dw_ref[k:k + 1, ls] * glu_buf[pl.ds(r0 + k * SUBLANES, chunk), ls]
            conv_buf[pl.ds(r0, chunk), ls] = acc
        return carry

    lax.fori_loop(0, rows // chunk, conv_chunk, 0)
    glu_buf[0:halo, :] = glu_buf[rows:rows + halo, :]

    c = conv_buf[...]
    mu = jnp.mean(c, axis=-1, keepdims=True)
    cc = c - mu
    var = jnp.mean(cc * cc, axis=-1, keepdims=True)
    y = cc * lax.rsqrt(var + EPS) * lg_ref[...] + lb_ref[...]
    y = (y * _sigmoid(y)).astype(_BF)
    o_ref[...] = x + _dot(y, w2_ref[...]) + b2_ref[...]


def _conformer(xt, norm, w1, b1, dw_w, dw_b, ln_g, ln_b, w2, b2, *, tt, chunk=64):
    n_rows, d = xt.shape
    rows = tt * SUBLANES
    halo = (dw_w.shape[0] - 1) * SUBLANES
    assert n_rows % rows == 0 and rows >= halo and rows % chunk == 0
    row = lambda v: v.reshape(1, -1)
    args = (xt, row(norm), w1.astype(_BF), row(b1), dw_w, row(dw_b), row(ln_g), row(ln_b),
            w2.astype(_BF), row(b2))
    tile = pl.BlockSpec((rows, d), lambda i: (i, 0))
    return pl.pallas_call(
        functools.partial(_conf_kernel, tt=tt, chunk=chunk),
        grid=(n_rows // rows,),
        in_specs=[tile] + [_const_spec(a.shape) for a in args[1:]],
        out_specs=tile,
        out_shape=jax.ShapeDtypeStruct((n_rows, d), _F32),
        scratch_shapes=[
            pltpu.VMEM((halo + rows, d), _F32),
            pltpu.VMEM((rows, d), _F32),
        ],
        compiler_params=_params(1),
        name="conformer",
    )(*args)


def _kv_kernel(m_ref, g_ref, wk_ref, wv_ref, k_ref, v_ref):
    m = _rms(m_ref[0], g_ref[...]).astype(_BF)
    k_ref[0] = _dot(m, wk_ref[...]).astype(_BF)
    v_ref[0] = _dot(m, wv_ref[...]).astype(_BF)


def _mem_kv(mem, mem_norm, wk, wv):
    b, m, d = mem.shape
    blk = pl.BlockSpec((1, m, d), lambda i: (i, 0, 0))
    return pl.pallas_call(
        _kv_kernel,
        grid=(b,),
        in_specs=[blk, _const_spec((1, d)), _const_spec((d, d)), _const_spec((d, d))],
        out_specs=[blk, blk],
        out_shape=[jax.ShapeDtypeStruct((b, m, d), _BF)] * 2,
        compiler_params=_params(1),
        name="mem_kv",
    )(mem, mem_norm.reshape(1, d), wk.astype(_BF), wv.astype(_BF))


def _xa_kernel(x_ref, g_ref, wq_ref, k_ref, v_ref, wo_ref, o_ref, *, n_heads):
    d = x_ref.shape[1]
    hd = d // n_heads
    x = x_ref[...]
    xn = _rms(x, g_ref[...]).astype(_BF)
    q = (_dot(xn, wq_ref[...]) * (hd ** -0.5)).astype(_BF)
    outs = []
    for h in range(n_heads):
        hs = slice(h * hd, (h + 1) * hd)
        s = lax.dot_general(q[:, hs], k_ref[0, :, hs], (((1,), (1,)), ((), ())),
                            preferred_element_type=_F32)
        e = jnp.exp(s - jnp.max(s, axis=-1, keepdims=True))
        inv = 1.0 / jnp.sum(e, axis=-1, keepdims=True)
        outs.append(_dot(e.astype(_BF), v_ref[0, :, hs]) * inv)
    o = jnp.concatenate(outs, axis=1).astype(_BF)
    o_ref[...] = x + _dot(o, wo_ref[...])


def _cross_attn(xt, k, v, norm, wq, wo, *, batch, n_heads, ts):
    n_rows, d = xt.shape
    seq = n_rows // batch
    hd = d // n_heads
    assert seq % ts == 0 and math.log2(hd) % 2 == 0
    m = k.shape[1]
    x2 = xt.reshape(seq, batch * d)
    tile = pl.BlockSpec((ts, d), lambda b, i: (i, b))
    kv = pl.BlockSpec((1, m, d), lambda b, i: (b, 0, 0))
    out = pl.pallas_call(
        functools.partial(_xa_kernel, n_heads=n_heads),
        grid=(batch, seq // ts),
        in_specs=[tile, _const_spec((1, d)), _const_spec((d, d)), kv, kv, _const_spec((d, d))],
        out_specs=tile,
        out_shape=jax.ShapeDtypeStruct((seq, batch * d), _F32),
        compiler_params=_params(2),
        name="cross_attn",
    )(x2, norm.reshape(1, d), wq.astype(_BF), k, v, wo.astype(_BF))
    return out.reshape(n_rows, d)


def _ffn_kernel(x_ref, g_ref, wup_ref, dw_ref, db_ref, wdn_ref, fin_ref, o_ref, g_halo,
                *, tt, fc, final_norm):
    i = pl.program_id(0)
    rows = tt * SUBLANES
    d_ff = wdn_ref.shape[0]
    conv_k = dw_ref.shape[0]
    halo = (conv_k - 1) * SUBLANES

    @pl.when(i == 0)
    def _():
        g_halo[...] = jnp.zeros_like(g_halo)

    x = x_ref[...]
    xn = _rms(x, g_ref[...]).astype(_BF)
    acc = x
    for c in range(d_ff // fc):
        cs = slice(c * fc, (c + 1) * fc)
        g = _dot(xn, wup_ref[:, cs])
        u = _dot(xn, wup_ref[:, d_ff + c * fc:d_ff + (c + 1) * fc])
        g_ext = jnp.concatenate([g_halo[:, cs], g], axis=0)
        g_halo[:, cs] = g[rows - halo:, :]
        conv = db_ref[:, cs] + dw_ref[0:1, cs] * g_ext[0:rows, :]
        for k in range(1, conv_k):
            conv = conv + dw_ref[k:k + 1, cs] * g_ext[k * SUBLANES:k * SUBLANES + rows, :]
        act = (_gelu_tanh(conv) * u).astype(_BF)
        acc = acc + _dot(act, wdn_ref[cs, :])
    if final_norm:
        acc = _rms(acc, fin_ref[...])
    o_ref[...] = acc


def _conv_ffn(xt, norm, w_up, dw_w, dw_b, w_down, fin, *, tt, fc=512, final_norm=False):
    n_rows, d = xt.shape
    d_ff = w_down.shape[0]
    rows = tt * SUBLANES
    halo = (dw_w.shape[0] - 1) * SUBLANES
    assert n_rows % rows == 0 and d_ff % fc == 0 and rows >= halo
    row = lambda v: v.reshape(1, -1)
    args = (xt, row(norm), w_up.astype(_BF), dw_w, row(dw_b), w_down.astype(_BF), row(fin))
    tile = pl.BlockSpec((rows, d), lambda i: (i, 0))
    return pl.pallas_call(
        functools.partial(_ffn_kernel, tt=tt, fc=fc, final_norm=final_norm),
        grid=(n_rows // rows,),
        in_specs=[tile] + [_const_spec(a.shape) for a in args[1:]],
        out_specs=tile,
        out_shape=jax.ShapeDtypeStruct((n_rows, d), _F32),
        scratch_shapes=[pltpu.VMEM((halo, d_ff), _F32)],
        compiler_params=_params(1),
        name="conv_ffn",
    )(*args)


def kernel(x, mem, ab_norm, ab_w_in, a_conv_w, a_conv_b, a_gate_x_w, a_gate_x_b, a_gate_a_w, a_gate_a_b, a_lambda, b_group_w, b_group_b, b_scale, ab_w_out, c_norm, c_w_pw1, c_b_pw1, c_dw_w, c_dw_b, c_ln_g, c_ln_b, c_w_pw2, c_b_pw2, xa_norm, xa_mem_norm, xa_wq, xa_wk, xa_wv, xa_wo, f_norm, f_w_up, f_dw_w, f_dw_b, f_w_down, final_norm):
    batch, seq, d = x.shape
    assert batch == SUBLANES
    depth = xa_norm.shape[0]
    n_heads = 4
    tt = 64
    xt = jnp.transpose(x, (1, 0, 2)).reshape(seq * batch, d)
    for layer in range(depth):
        if layer % 2 == 0:
            i = layer // 2
            xt = _mixer_ab(xt, ab_norm[i], ab_w_in[i], a_conv_w[i], a_conv_b[i], a_gate_x_w[i],
                           a_gate_x_b[i], a_gate_a_w[i], a_gate_a_b[i], a_lambda[i],
                           b_group_w[i], b_group_b[i], b_scale[i], ab_w_out[i], tt=tt)
        else:
            j = layer // 2
            xt = _conformer(xt, c_norm[j], c_w_pw1[j], c_b_pw1[j], c_dw_w[j], c_dw_b[j],
                            c_ln_g[j], c_ln_b[j], c_w_pw2[j], c_b_pw2[j], tt=tt)
        k, v = _mem_kv(mem, xa_mem_norm[layer], xa_wk[layer], xa_wv[layer])
        xt = _cross_attn(xt, k, v, xa_norm[layer], xa_wq[layer], xa_wo[layer],
                         batch=batch, n_heads=n_heads, ts=512)
        xt = _conv_ffn(xt, f_norm[layer], f_w_up[layer], f_dw_w[layer], f_dw_b[layer],
                       f_w_down[layer], final_norm, tt=tt, final_norm=(layer == depth - 1))
    return jnp.transpose(xt.reshape(seq, batch, d), (1, 0, 2))
```

```python
import functools
import math

import jax
import jax.numpy as jnp
from jax import lax
from jax.experimental import pallas as pl
from jax.experimental.pallas import tpu as pltpu

EPS = 1e-6
C_RG = 8.0
POOL_WINDOWS = (2, 4, 8, 16)
XA_HEADS = 4
SUBLANES = 8
LANES = 128
VMEM_LIMIT_BYTES = 56 * 1024 * 1024

_BF = jnp.bfloat16
_F32 = jnp.float32


def _dot(a, b):
    return jnp.dot(a, b, preferred_element_type=_F32)


def _rms(x, g):
    return x * lax.rsqrt(jnp.mean(x * x, axis=-1, keepdims=True) + EPS) * g


def _sigmoid(x):
    return 1.0 / (1.0 + jnp.exp(-x))


def _gelu_tanh(x):
    c = math.sqrt(2.0 / math.pi)
    return 0.5 * x * (1.0 + jnp.tanh(c * (x + 0.044715 * (x * x * x))))


def _softplus(x):
    return jnp.maximum(x, 0.0) + jnp.log1p(jnp.exp(-jnp.abs(x)))


def _load_time_major(x_ref):
    b, tt, d = x_ref.shape
    return jnp.swapaxes(x_ref[...], 0, 1).reshape(tt * b, d)


def _store_batch_major(o_ref, v):
    b, tt, d = o_ref.shape
    o_ref[...] = jnp.swapaxes(v.reshape(tt, b, d), 0, 1)


def _const_spec(shape):
    nd = len(shape)
    return pl.BlockSpec(shape, lambda *_: (0,) * nd, pipeline_mode=pl.Buffered(1))


def _params(n_grid):
    return pltpu.CompilerParams(
        dimension_semantics=("arbitrary",) * n_grid,
        vmem_limit_bytes=VMEM_LIMIT_BYTES,
    )


def _time_tiled_call(body, x, consts, scratch, *, tt, name):
    batch, seq, d = x.shape
    assert batch == SUBLANES and seq % tt == 0
    tile = pl.BlockSpec((batch, tt, d), lambda i: (0, i, 0))
    return pl.pallas_call(
        body,
        grid=(seq // tt,),
        in_specs=[tile] + [_const_spec(a.shape) for a in consts],
        out_specs=tile,
        out_shape=jax.ShapeDtypeStruct(x.shape, _F32),
        scratch_shapes=scratch,
        compiler_params=_params(1),
        name=name,
    )(x, *consts)


def _row(v):
    return v.reshape(1, -1)


def _ab_kernel(x_ref, g_ref, win_ref, cw_ref, cb_ref, wg_ref, bgx_ref, bga_ref, lam_ref,
               wp_ref, bp_ref, sc_ref, wout_ref, o_ref,
               zrec_buf, pool_buf, a_buf, bx_buf, h_buf, h_carry, *, wa, wb, n_heads):
    i = pl.program_id(0)
    tt = x_ref.shape[1]
    rows = tt * SUBLANES
    conv_k = cw_ref.shape[0]
    halo_c = (conv_k - 1) * SUBLANES
    halo_p = POOL_WINDOWS[-1] * SUBLANES
    hd = wa // n_heads
    hdb = wb // len(POOL_WINDOWS)

    @pl.when(i == 0)
    def _():
        zrec_buf[0:halo_c, :] = jnp.zeros((halo_c, wa), _F32)
        pool_buf[0:halo_p, :] = jnp.zeros((halo_p, wb), _F32)
        h_carry[...] = jnp.zeros_like(h_carry)

    x = _load_time_major(x_ref)
    xn = _rms(x, g_ref[...]).astype(_BF)

    zrec_buf[halo_c:halo_c + rows, :] = _dot(xn, win_ref[:, wa:2 * wa])
    xr = cb_ref[...] + cw_ref[0:1, :] * zrec_buf[0:rows, :]
    for k in range(1, conv_k):
        xr = xr + cw_ref[k:k + 1, :] * zrec_buf[k * SUBLANES:k * SUBLANES + rows, :]
    zrec_buf[0:halo_c, :] = zrec_buf[rows:rows + halo_c, :]

    xr_b = xr.astype(_BF)
    sp = _softplus(-lam_ref[...])
    for h in range(n_heads):
        hs = slice(h * hd, (h + 1) * hd)
        gg = _dot(xr_b[:, hs], wg_ref[h])
        gate_x = _sigmoid(gg[:, :hd] + bgx_ref[:, hs])
        gate_a = _sigmoid(gg[:, hd:] + bga_ref[:, hs])
        a = jnp.exp((-C_RG) * gate_a * sp[:, hs])
        a_buf[:, hs] = a
        bx_buf[:, hs] = jnp.sqrt(1.0 - a * a) * (gate_x * xr[:, hs])

    h = h_carry[...]
    for t in range(tt):
        ts = slice(t * SUBLANES, (t + 1) * SUBLANES)
        h = a_buf[ts, :] * h + bx_buf[ts, :]
        h_buf[ts, :] = h
    h_carry[...] = h

    y_a = (_gelu_tanh(_dot(xn, win_ref[:, 0:wa])) * h_buf[...]).astype(_BF)

    z_pool = _dot(xn, win_ref[:, 2 * wa:2 * wa + wb])
    pool_buf[halo_p:halo_p + rows, :] = z_pool
    t_idx = i * tt + lax.broadcasted_iota(jnp.int32, (rows, 1), 0) // SUBLANES
    yb = []
    for g, w in enumerate(POOL_WINDOWS):
        gs = slice(g * hdb, (g + 1) * hdb)
        cur = pool_buf[:, gs]
        shift = 1
        while shift < w:
            cur = cur[shift * SUBLANES:, :] + cur[:cur.shape[0] - shift * SUBLANES, :]
            shift *= 2
        win_sum = cur[cur.shape[0] - rows:, :]
        cnt = jnp.minimum(t_idx + 1, w).astype(_F32)
        p = (win_sum / cnt - z_pool[:, gs]).astype(_BF)
        yb.append((_dot(p, wp_ref[g]) + bp_ref[:, gs]) * sc_ref[:, gs])
    pool_buf[0:halo_p, :] = pool_buf[rows:rows + halo_p, :]
    y_b = jnp.concatenate(yb, axis=1).astype(_BF)

    out = x + _dot(y_a, wout_ref[0:wa, :]) + _dot(y_b, wout_ref[wa:wa + wb, :])
    _store_batch_major(o_ref, out)


def _mixer_ab(x, norm, w_in, conv_w, conv_b, w_gx, b_gx, w_ga, b_ga, lam, w_pool, b_pool,
              pool_scale, w_out, *, tt):
    d = x.shape[2]
    n_heads, hd, _ = w_gx.shape
    wa = n_heads * hd
    wb = w_pool.shape[0] * w_pool.shape[1]
    rows = tt * SUBLANES
    halo_c = (conv_w.shape[0] - 1) * SUBLANES
    halo_p = POOL_WINDOWS[-1] * SUBLANES
    assert rows >= halo_p and w_in.shape == (d, 2 * wa + wb)
    wg = jnp.concatenate([w_gx, w_ga], axis=-1).astype(_BF)
    consts = (_row(norm), w_in.astype(_BF), conv_w, _row(conv_b), wg, _row(b_gx), _row(b_ga),
              _row(lam), w_pool.astype(_BF), _row(b_pool), _row(pool_scale), w_out.astype(_BF))
    scratch = [
        pltpu.VMEM((halo_c + rows, wa), _F32),
        pltpu.VMEM((halo_p + rows, wb), _F32),
        pltpu.VMEM((rows, wa), _F32),
        pltpu.VMEM((rows, wa), _F32),
        pltpu.VMEM((rows, wa), _F32),
        pltpu.VMEM((SUBLANES, wa), _F32),
    ]
    body = functools.partial(_ab_kernel, wa=wa, wb=wb, n_heads=n_heads)
    return _time_tiled_call(body, x, consts, scratch, tt=tt, name="mixer_ab")


def _conf_kernel(x_ref, g_ref, w1_ref, b1_ref, dw_ref, db_ref, lg_ref, lb_ref, w2_ref, b2_ref,
                 o_ref, glu_buf, conv_buf, *, chunk):
    i = pl.program_id(0)
    tt, d = x_ref.shape[1], x_ref.shape[2]
    rows = tt * SUBLANES
    conv_k = dw_ref.shape[0]
    halo = (conv_k - 1) * SUBLANES

    @pl.when(i == 0)
    def _():
        glu_buf[0:halo, :] = jnp.zeros((halo, d), _F32)

    x = _load_time_major(x_ref)
    xn = _rms(x, g_ref[...]).astype(_BF)
    ha = _dot(xn, w1_ref[:, 0:d]) + b1_ref[:, 0:d]
    hb = _dot(xn, w1_ref[:, d:2 * d]) + b1_ref[:, d:2 * d]
    glu_buf[halo:halo + rows, :] = ha * _sigmoid(hb)

    def conv_chunk(c, carry):
        r0 = pl.multiple_of(c * chunk, chunk)
        for j in range(d // LANES):
            ls = slice(j * LANES, (j + 1) * LANES)
            acc = db_ref[:, ls] + dw_ref[0:1, ls] * glu_buf[pl.ds(r0, chunk), ls]
            for k in range(1, conv_k):
                acc = acc + dw_ref[k:k + 1, ls] * glu_buf[pl.ds(r0 + k * SUBLANES, chunk), ls]
            conv_buf[pl.ds(r0, chunk), ls] = acc
        return carry

    lax.fori_loop(0, rows // chunk, conv_chunk, 0)
    glu_buf[0:halo, :] = glu_buf[rows:rows + halo, :]

    c = conv_buf[...]
    mu = jnp.mean(c, axis=-1, keepdims=True)
    cc = c - mu
    var = jnp.mean(cc * cc, axis=-1, keepdims=True)
    y = cc * lax.rsqrt(var + EPS) * lg_ref[...] + lb_ref[...]
    y = (y * _sigmoid(y)).astype(_BF)
    _store_batch_major(o_ref, x + _dot(y, w2_ref[...]) + b2_ref[...])


def _conformer(x, norm, w1, b1, dw_w, dw_b, ln_g, ln_b, w2, b2, *, tt, chunk=64):
    d = x.shape[2]
    rows = tt * SUBLANES
    halo = (dw_w.shape[0] - 1) * SUBLANES
    assert rows >= halo and rows % chunk == 0
    consts = (_row(norm), w1.astype(_BF), _row(b1), dw_w, _row(dw_b), _row(ln_g), _row(ln_b),
              w2.astype(_BF), _row(b2))
    scratch = [pltpu.VMEM((halo + rows, d), _F32), pltpu.VMEM((rows, d), _F32)]
    body = functools.partial(_conf_kernel, chunk=chunk)
    return _time_tiled_call(body, x, consts, scratch, tt=tt, name="conformer")


def _kv_kernel(m_ref, g_ref, wk_ref, wv_ref, k_ref, v_ref):
    m = _rms(m_ref[0], g_ref[...]).astype(_BF)
    k_ref[0] = _dot(m, wk_ref[...]).astype(_BF)
    v_ref[0] = _dot(m, wv_ref[...]).astype(_BF)


def _mem_kv(mem, mem_norm, wk, wv):
    b, m, d = mem.shape
    blk = pl.BlockSpec((1, m, d), lambda i: (i, 0, 0))
    return pl.pallas_call(
        _kv_kernel,
        grid=(b,),
        in_specs=[blk, _const_spec((1, d)), _const_spec((d, d)), _const_spec((d, d))],
        out_specs=[blk, blk],
        out_shape=[jax.ShapeDtypeStruct((b, m, d), _BF)] * 2,
        compiler_params=_params(1),
        name="mem_kv",
    )(mem, _row(mem_norm), wk.astype(_BF), wv.astype(_BF))


def _xa_kernel(x_ref, g_ref, wq_ref, k_ref, v_ref, wo_ref, o_ref, *, n_heads):
    d = x_ref.shape[2]
    hd = d // n_heads
    x = x_ref[0]
    xn = _rms(x, g_ref[...]).astype(_BF)
    q = (_dot(xn, wq_ref[...]) * (hd ** -0.5)).astype(_BF)
    outs = []
    for h in range(n_heads):
        hs = slice(h * hd, (h + 1) * hd)
        s = lax.dot_general(q[:, hs], k_ref[0, :, hs], (((1,), (1,)), ((), ())),
                            preferred_element_type=_F32)
        e = jnp.exp(s - jnp.max(s, axis=-1, keepdims=True))
        inv = 1.0 / jnp.sum(e, axis=-1, keepdims=True)
        outs.append(_dot(e.astype(_BF), v_ref[0, :, hs]) * inv)
    o = jnp.concatenate(outs, axis=1).astype(_BF)
    o_ref[0] = x + _dot(o, wo_ref[...])


def _cross_attn(x, k, v, norm, wq, wo, *, n_heads, ts):
    batch, seq, d = x.shape
    hd = d // n_heads
    assert seq % ts == 0 and math.log2(hd) % 2 == 0
    m = k.shape[1]
    tile = pl.BlockSpec((1, ts, d), lambda b, i: (b, i, 0))
    kv = pl.BlockSpec((1, m, d), lambda b, i: (b, 0, 0))
    return pl.pallas_call(
        functools.partial(_xa_kernel, n_heads=n_heads),
        grid=(batch, seq // ts),
        in_specs=[tile, _const_spec((1, d)), _const_spec((d, d)), kv, kv, _const_spec((d, d))],
        out_specs=tile,
        out_shape=jax.ShapeDtypeStruct(x.shape, _F32),
        compiler_params=_params(2),
        name="cross_attn",
    )(x, _row(norm), wq.astype(_BF), k, v, wo.astype(_BF))


def _ffn_kernel(x_ref, g_ref, wup_ref, dw_ref, db_ref, wdn_ref, fin_ref, o_ref, g_halo,
                *, fc, final_norm):
    i = pl.program_id(0)
    tt = x_ref.shape[1]
    rows = tt * SUBLANES
    d_ff = wdn_ref.shape[0]
    conv_k = dw_ref.shape[0]
    halo = (conv_k - 1) * SUBLANES

    @pl.when(i == 0)
    def _():
        g_halo[...] = jnp.zeros_like(g_halo)

    x = _load_time_major(x_ref)
    xn = _rms(x, g_ref[...]).astype(_BF)
    acc = x
    for c in range(d_ff // fc):
        cs = slice(c * fc, (c + 1) * fc)
        g = _dot(xn, wup_ref[:, cs])
        u = _dot(xn, wup_ref[:, d_ff + c * fc:d_ff + (c + 1) * fc])
        g_ext = jnp.concatenate([g_halo[:, cs], g], axis=0)
        g_halo[:, cs] = g[rows - halo:, :]
        conv = db_ref[:, cs] + dw_ref[0:1, cs] * g_ext[0:rows, :]
        for k in range(1, conv_k):
            conv = conv + dw_ref[k:k + 1, cs] * g_ext[k * SUBLANES:k * SUBLANES + rows, :]
        act = (_gelu_tanh(conv) * u).astype(_BF)
        acc = acc + _dot(act, wdn_ref[cs, :])
    if final_norm:
        acc = _rms(acc, fin_ref[...])
    _store_batch_major(o_ref, acc)


def _conv_ffn(x, norm, w_up, dw_w, dw_b, w_down, fin, *, tt, fc=512, final_norm=False):
    d_ff = w_down.shape[0]
    rows = tt * SUBLANES
    halo = (dw_w.shape[0] - 1) * SUBLANES
    assert d_ff % fc == 0 and rows >= halo
    consts = (_row(norm), w_up.astype(_BF), dw_w, _row(dw_b), w_down.astype(_BF), _row(fin))
    scratch = [pltpu.VMEM((halo, d_ff), _F32)]
    body = functools.partial(_ffn_kernel, fc=fc, final_norm=final_norm)
    return _time_tiled_call(body, x, consts, scratch, tt=tt, name="conv_ffn")


def kernel(x, mem, ab_norm, ab_w_in, a_conv_w, a_conv_b, a_gate_x_w, a_gate_x_b, a_gate_a_w, a_gate_a_b, a_lambda, b_group_w, b_group_b, b_scale, ab_w_out, c_norm, c_w_pw1, c_b_pw1, c_dw_w, c_dw_b, c_ln_g, c_ln_b, c_w_pw2, c_b_pw2, xa_norm, xa_mem_norm, xa_wq, xa_wk, xa_wv, xa_wo, f_norm, f_w_up, f_dw_w, f_dw_b, f_w_down, final_norm):
    depth = xa_norm.shape[0]
    tt = 64
    for layer in range(depth):
        if layer % 2 == 0:
            i = layer // 2
            x = _mixer_ab(x, ab_norm[i], ab_w_in[i], a_conv_w[i], a_conv_b[i], a_gate_x_w[i],
                          a_gate_x_b[i], a_gate_a_w[i], a_gate_a_b[i], a_lambda[i],
                          b_group_w[i], b_group_b[i], b_scale[i], ab_w_out[i], tt=tt)
        else:
            j = layer // 2
            x = _conformer(x, c_norm[j], c_w_pw1[j], c_b_pw1[j], c_dw_w[j], c_dw_b[j],
                           c_ln_g[j], c_ln_b[j], c_w_pw2[j], c_b_pw2[j], tt=tt)
        k, v = _mem_kv(mem, xa_mem_norm[layer], xa_wk[layer], xa_wv[layer])
        x = _cross_attn(x, k, v, xa_norm[layer], xa_wq[layer], xa_wo[layer],
                        n_heads=XA_HEADS, ts=512)
        x = _conv_ffn(x, f_norm[layer], f_w_up[layer], f_dw_w[layer], f_dw_b[layer],
                      f_w_down[layer], final_norm, tt=tt, final_norm=(layer == depth - 1))
    return x
```

```python
import functools
import math

import jax
import jax.numpy as jnp
from jax import lax
from jax.experimental import pallas as pl
from jax.experimental.pallas import tpu as pltpu

EPS = 1e-6
C_RG = 8.0
POOL_WINDOWS = (2, 4, 8, 16)
XA_HEADS = 4
SUBLANES = 8
LANES = 128
VMEM_LIMIT_BYTES = 56 * 1024 * 1024

_BF = jnp.bfloat16
_F32 = jnp.float32


def _dot(a, b):
    return jnp.dot(a, b, preferred_element_type=_F32)


def _rms(x, g):
    return x * lax.rsqrt(jnp.mean(x * x, axis=-1, keepdims=True) + EPS) * g


def _sigmoid(x):
    return 1.0 / (1.0 + jnp.exp(-x))


def _gelu_tanh(x):
    c = math.sqrt(2.0 / math.pi)
    return 0.5 * x * (1.0 + jnp.tanh(c * (x + 0.044715 * (x * x * x))))


def _softplus(x):
    return jnp.maximum(x, 0.0) + jnp.log1p(jnp.exp(-jnp.abs(x)))


def _load_time_major(x_ref):
    tt, b, d = x_ref.shape
    return x_ref[...].reshape(tt * b, d)


def _store_time_major(o_ref, v):
    o_ref[...] = v.reshape(o_ref.shape)


def _const_spec(shape):
    nd = len(shape)
    return pl.BlockSpec(shape, lambda *_: (0,) * nd, pipeline_mode=pl.Buffered(1))


def _params(n_grid):
    return pltpu.CompilerParams(
        dimension_semantics=("arbitrary",) * n_grid,
        vmem_limit_bytes=VMEM_LIMIT_BYTES,
    )


def _time_tiled_call(body, x, consts, scratch, *, tt, name):
    batch, seq, d = x.shape
    n_steps = seq // tt
    assert batch == SUBLANES and seq % tt == 0 and n_steps >= 2
    n_c, n_s = len(consts), len(scratch)

    def kern(*refs):
        x_hbm, c_refs, o_hbm = refs[0], refs[1:1 + n_c], refs[1 + n_c]
        s_refs = refs[2 + n_c:2 + n_c + n_s]
        xbuf, obuf, sem_in, sem_out = refs[2 + n_c + n_s:]
        i = pl.program_id(0)
        slot = lax.rem(i, 2)

        def copies_in(step, sl):
            return [pltpu.make_async_copy(x_hbm.at[b, pl.ds(step * tt, tt), :],
                                          xbuf.at[sl, :, b, :], sem_in.at[sl]) for b in range(batch)]

        def copies_out(step, sl):
            return [pltpu.make_async_copy(obuf.at[sl, :, b, :],
                                          o_hbm.at[b, pl.ds(step * tt, tt), :], sem_out.at[sl])
                    for b in range(batch)]

        @pl.when(i == 0)
        def _():
            for c in copies_in(0, 0):
                c.start()

        @pl.when(i + 1 < n_steps)
        def _():
            for c in copies_in(i + 1, 1 - slot):
                c.start()

        for c in copies_in(i, slot):
            c.wait()

        @pl.when(i >= 2)
        def _():
            for c in copies_out(i - 2, slot):
                c.wait()

        body(xbuf.at[slot], *c_refs, obuf.at[slot], *s_refs)

        for c in copies_out(i, slot):
            c.start()

        @pl.when(i == n_steps - 1)
        def _():
            for c in copies_out(i - 1, 1 - slot) + copies_out(i, slot):
                c.wait()

    any_spec = pl.BlockSpec(memory_space=pl.ANY)
    return pl.pallas_call(
        kern,
        grid=(n_steps,),
        in_specs=[any_spec] + [_const_spec(a.shape) for a in consts],
        out_specs=any_spec,
        out_shape=jax.ShapeDtypeStruct(x.shape, _F32),
        scratch_shapes=list(scratch) + [
            pltpu.VMEM((2, tt, batch, d), _F32),
            pltpu.VMEM((2, tt, batch, d), _F32),
            pltpu.SemaphoreType.DMA((2,)),
            pltpu.SemaphoreType.DMA((2,)),
        ],
        compiler_params=_params(1),
        name=name,
    )(x, *consts)


def _row(v):
    return v.reshape(1, -1)


def _ab_kernel(x_ref, g_ref, win_ref, cw_ref, cb_ref, wg_ref, bgx_ref, bga_ref, lam_ref,
               wp_ref, bp_ref, sc_ref, wout_ref, o_ref,
               zrec_buf, pool_buf, a_buf, bx_buf, h_buf, h_carry, *, wa, wb, n_heads):
    i = pl.program_id(0)
    tt = x_ref.shape[0]
    rows = tt * SUBLANES
    conv_k = cw_ref.shape[0]
    halo_c = (conv_k - 1) * SUBLANES
    halo_p = POOL_WINDOWS[-1] * SUBLANES
    hd = wa // n_heads
    hdb = wb // len(POOL_WINDOWS)

    @pl.when(i == 0)
    def _():
        zrec_buf[0:halo_c, :] = jnp.zeros((halo_c, wa), _F32)
        pool_buf[0:halo_p, :] = jnp.zeros((halo_p, wb), _F32)
        h_carry[...] = jnp.zeros_like(h_carry)

    x = _load_time_major(x_ref)
    xn = _rms(x, g_ref[...]).astype(_BF)

    zrec_buf[halo_c:halo_c + rows, :] = _dot(xn, win_ref[:, wa:2 * wa])
    xr = cb_ref[...] + cw_ref[0:1, :] * zrec_buf[0:rows, :]
    for k in range(1, conv_k):
        xr = xr + cw_ref[k:k + 1, :] * zrec_buf[k * SUBLANES:k * SUBLANES + rows, :]
    zrec_buf[0:halo_c, :] = zrec_buf[rows:rows + halo_c, :]

    xr_b = xr.astype(_BF)
    sp = _softplus(-lam_ref[...])
    for h in range(n_heads):
        hs = slice(h * hd, (h + 1) * hd)
        gg = _dot(xr_b[:, hs], wg_ref[h])
        gate_x = _sigmoid(gg[:, :hd] + bgx_ref[:, hs])
        gate_a = _sigmoid(gg[:, hd:] + bga_ref[:, hs])
        a = jnp.exp((-C_RG) * gate_a * sp[:, hs])
        a_buf[:, hs] = a
        bx_buf[:, hs] = jnp.sqrt(1.0 - a * a) * (gate_x * xr[:, hs])

    h = h_carry[...]
    for t in range(tt):
        ts = slice(t * SUBLANES, (t + 1) * SUBLANES)
        h = a_buf[ts, :] * h + bx_buf[ts, :]
        h_buf[ts, :] = h
    h_carry[...] = h

    y_a = (_gelu_tanh(_dot(xn, win_ref[:, 0:wa])) * h_buf[...]).astype(_BF)

    z_pool = _dot(xn, win_ref[:, 2 * wa:2 * wa + wb])
    pool_buf[halo_p:halo_p + rows, :] = z_pool
    t_idx = i * tt + lax.broadcasted_iota(jnp.int32, (rows, 1), 0) // SUBLANES
    yb = []
    for g, w in enumerate(POOL_WINDOWS):
        gs = slice(g * hdb, (g + 1) * hdb)
        cur = pool_buf[:, gs]
        shift = 1
        while shift < w:
            cur = cur[shift * SUBLANES:, :] + cur[:cur.shape[0] - shift * SUBLANES, :]
            shift *= 2
        win_sum = cur[cur.shape[0] - rows:, :]
        cnt = jnp.minimum(t_idx + 1, w).astype(_F32)
        p = (win_sum / cnt - z_pool[:, gs]).astype(_BF)
        yb.append((_dot(p, wp_ref[g]) + bp_ref[:, gs]) * sc_ref[:, gs])
    pool_buf[0:halo_p, :] = pool_buf[rows:rows + halo_p, :]
    y_b = jnp.concatenate(yb, axis=1).astype(_BF)

    out = x + _dot(y_a, wout_ref[0:wa, :]) + _dot(y_b, wout_ref[wa:wa + wb, :])
    _store_time_major(o_ref, out)


def _mixer_ab(x, norm, w_in, conv_w, conv_b, w_gx, b_gx, w_ga, b_ga, lam, w_pool, b_pool,
              pool_scale, w_out, *, tt):
    d = x.shape[2]
    n_heads, hd, _ = w_gx.shape
    wa = n_heads * hd
    wb = w_pool.shape[0] * w_pool.shape[1]
    rows = tt * SUBLANES
    halo_c = (conv_w.shape[0] - 1) * SUBLANES
    halo_p = POOL_WINDOWS[-1] * SUBLANES
    assert rows >= halo_p and w_in.shape == (d, 2 * wa + wb)
    wg = jnp.concatenate([w_gx, w_ga], axis=-1).astype(_BF)
    consts = (_row(norm), w_in.astype(_BF), conv_w, _row(conv_b), wg, _row(b_gx), _row(b_ga),
              _row(lam), w_pool.astype(_BF), _row(b_pool), _row(pool_scale), w_out.astype(_BF))
    scratch = [
        pltpu.VMEM((halo_c + rows, wa), _F32),
        pltpu.VMEM((halo_p + rows, wb), _F32),
        pltpu.VMEM((rows, wa), _F32),
        pltpu.VMEM((rows, wa), _F32),
        pltpu.VMEM((rows, wa), _F32),
        pltpu.VMEM((SUBLANES, wa), _F32),
    ]
    body = functools.partial(_ab_kernel, wa=wa, wb=wb, n_heads=n_heads)
    return _time_tiled_call(body, x, consts, scratch, tt=tt, name="mixer_ab")


def _conf_kernel(x_ref, g_ref, w1_ref, b1_ref, dw_ref, db_ref, lg_ref, lb_ref, w2_ref, b2_ref,
                 o_ref, glu_buf, conv_buf, *, chunk):
    i = pl.program_id(0)
    tt, d = x_ref.shape[0], x_ref.shape[2]
    rows = tt * SUBLANES
    conv_k = dw_ref.shape[0]
    halo = (conv_k - 1) * SUBLANES

    @pl.when(i == 0)
    def _():
        glu_buf[0:halo, 0:d] = jnp.zeros((halo, d), _F32)

    x = _load_time_major(x_ref)
    xn = _rms(x, g_ref[...]).astype(_BF)
    ha = _dot(xn, w1_ref[:, 0:d]) + b1_ref[:, 0:d]
    hb = _dot(xn, w1_ref[:, d:2 * d]) + b1_ref[:, d:2 * d]
    glu_buf[halo:halo + rows, 0:d] = ha * _sigmoid(hb)

    def conv_chunk(c, carry):
        r0 = pl.multiple_of(c * chunk, chunk)
        for j in range(d // LANES):
            ls = slice(j * LANES, (j + 1) * LANES)
            acc = db_ref[:, ls] + dw_ref[0:1, ls] * glu_buf[pl.ds(r0, chunk), ls]
            for k in range(1, conv_k):
                acc = acc + dw_ref[k:k + 1, ls] * glu_buf[pl.ds(r0 + k * SUBLANES, chunk), ls]
            conv_buf[pl.ds(r0, chunk), ls] = acc
        return carry

    lax.fori_loop(0, rows // chunk, conv_chunk, 0)
    glu_buf[0:halo, 0:d] = glu_buf[rows:rows + halo, 0:d]

    c = conv_buf[...]
    mu = jnp.mean(c, axis=-1, keepdims=True)
    cc = c - mu
    var = jnp.mean(cc * cc, axis=-1, keepdims=True)
    y = cc * lax.rsqrt(var + EPS) * lg_ref[...] + lb_ref[...]
    y = (y * _sigmoid(y)).astype(_BF)
    _store_time_major(o_ref, x + _dot(y, w2_ref[...]) + b2_ref[...])


def _conformer(x, norm, w1, b1, dw_w, dw_b, ln_g, ln_b, w2, b2, *, tt, chunk=64):
    d = x.shape[2]
    rows = tt * SUBLANES
    halo = (dw_w.shape[0] - 1) * SUBLANES
    assert rows >= halo and rows % chunk == 0
    consts = (_row(norm), w1.astype(_BF), _row(b1), dw_w, _row(dw_b), _row(ln_g), _row(ln_b),
              w2.astype(_BF), _row(b2))
    scratch = [pltpu.VMEM((halo + rows, d + LANES), _F32), pltpu.VMEM((rows, d), _F32)]
    body = functools.partial(_conf_kernel, chunk=chunk)
    return _time_tiled_call(body, x, consts, scratch, tt=tt, name="conformer")


def _kv_kernel(m_ref, g_ref, wk_ref, wv_ref, k_ref, v_ref):
    m = _rms(m_ref[0], g_ref[...]).astype(_BF)
    k_ref[0] = _dot(m, wk_ref[...]).astype(_BF)
    v_ref[0] = _dot(m, wv_ref[...]).astype(_BF)


def _mem_kv(mem, mem_norm, wk, wv):
    b, m, d = mem.shape
    blk = pl.BlockSpec((1, m, d), lambda i: (i, 0, 0))
    return pl.pallas_call(
        _kv_kernel,
        grid=(b,),
        in_specs=[blk, _const_spec((1, d)), _const_spec((d, d)), _const_spec((d, d))],
        out_specs=[blk, blk],
        out_shape=[jax.ShapeDtypeStruct((b, m, d), _BF)] * 2,
        compiler_params=_params(1),
        name="mem_kv",
    )(mem, _row(mem_norm), wk.astype(_BF), wv.astype(_BF))


def _xa_kernel(x_ref, g_ref, wq_ref, k_ref, v_ref, wo_ref, o_ref, *, n_heads):
    d = x_ref.shape[2]
    hd = d // n_heads
    x = x_ref[0]
    xn = _rms(x, g_ref[...]).astype(_BF)
    q = (_dot(xn, wq_ref[...]) * (hd ** -0.5)).astype(_BF)
    outs = []
    for h in range(n_heads):
        hs = slice(h * hd, (h + 1) * hd)
        s = lax.dot_general(q[:, hs], k_ref[0, :, hs], (((1,), (1,)), ((), ())),
                            preferred_element_type=_F32)
        e = jnp.exp(s - jnp.max(s, axis=-1, keepdims=True))
        inv = 1.0 / jnp.sum(e, axis=-1, keepdims=True)
        outs.append(_dot(e.astype(_BF), v_ref[0, :, hs]) * inv)
    o = jnp.concatenate(outs, axis=1).astype(_BF)
    o_ref[0] = x + _dot(o, wo_ref[...])


def _cross_attn(x, k, v, norm, wq, wo, *, n_heads, ts):
    batch, seq, d = x.shape
    hd = d // n_heads
    assert seq % ts == 0 and math.log2(hd) % 2 == 0
    m = k.shape[1]
    tile = pl.BlockSpec((1, ts, d), lambda b, i: (b, i, 0))
    kv = pl.BlockSpec((1, m, d), lambda b, i: (b, 0, 0))
    return pl.pallas_call(
        functools.partial(_xa_kernel, n_heads=n_heads),
        grid=(batch, seq // ts),
        in_specs=[tile, _const_spec((1, d)), _const_spec((d, d)), kv, kv, _const_spec((d, d))],
        out_specs=tile,
        out_shape=jax.ShapeDtypeStruct(x.shape, _F32),
        compiler_params=_params(2),
        name="cross_attn",
    )(x, _row(norm), wq.astype(_BF), k, v, wo.astype(_BF))


def _ffn_kernel(x_ref, g_ref, wup_ref, dw_ref, db_ref, wdn_ref, fin_ref, o_ref, g_halo,
                *, fc, final_norm):
    i = pl.program_id(0)
    tt = x_ref.shape[0]
    rows = tt * SUBLANES
    d_ff = wdn_ref.shape[0]
    conv_k = dw_ref.shape[0]
    halo = (conv_k - 1) * SUBLANES

    @pl.when(i == 0)
    def _():
        g_halo[...] = jnp.zeros_like(g_halo)

    x = _load_time_major(x_ref)
    xn = _rms(x, g_ref[...]).astype(_BF)
    def up(c):
        return (_dot(xn, wup_ref[:, c * fc:(c + 1) * fc]),
                _dot(xn, wup_ref[:, d_ff + c * fc:d_ff + (c + 1) * fc]))

    acc = x
    n_chunks = d_ff // fc
    nxt = up(0)
    for c in range(n_chunks):
        cs = slice(c * fc, (c + 1) * fc)
        g, u = nxt
        if c + 1 < n_chunks:
            nxt = up(c + 1)
        g_ext = jnp.concatenate([g_halo[:, cs], g], axis=0)
        g_halo[:, cs] = g[rows - halo:, :]
        conv = db_ref[:, cs] + dw_ref[0:1, cs] * g_ext[0:rows, :]
        for k in range(1, conv_k):
            conv = conv + dw_ref[k:k + 1, cs] * g_ext[k * SUBLANES:k * SUBLANES + rows, :]
        act = (_gelu_tanh(conv) * u).astype(_BF)
        acc = acc + _dot(act, wdn_ref[cs, :])
    if final_norm:
        acc = _rms(acc, fin_ref[...])
    _store_time_major(o_ref, acc)


def _conv_ffn(x, norm, w_up, dw_w, dw_b, w_down, fin, *, tt, fc=512, final_norm=False):
    d_ff = w_down.shape[0]
    rows = tt * SUBLANES
    halo = (dw_w.shape[0] - 1) * SUBLANES
    assert d_ff % fc == 0 and rows >= halo
    consts = (_row(norm), w_up.astype(_BF), dw_w, _row(dw_b), w_down.astype(_BF), _row(fin))
    scratch = [pltpu.VMEM((halo, d_ff), _F32)]
    body = functools.partial(_ffn_kernel, fc=fc, final_norm=final_norm)
    return _time_tiled_call(body, x, consts, scratch, tt=tt, name="conv_ffn")


def kernel(x, mem, ab_norm, ab_w_in, a_conv_w, a_conv_b, a_gate_x_w, a_gate_x_b, a_gate_a_w, a_gate_a_b, a_lambda, b_group_w, b_group_b, b_scale, ab_w_out, c_norm, c_w_pw1, c_b_pw1, c_dw_w, c_dw_b, c_ln_g, c_ln_b, c_w_pw2, c_b_pw2, xa_norm, xa_mem_norm, xa_wq, xa_wk, xa_wv, xa_wo, f_norm, f_w_up, f_dw_w, f_dw_b, f_w_down, final_norm):
    depth = xa_norm.shape[0]
    tt = 64
    for layer in range(depth):
        if layer % 2 == 0:
            i = layer // 2
            x = _mixer_ab(x, ab_norm[i], ab_w_in[i], a_conv_w[i], a_conv_b[i], a_gate_x_w[i],
                          a_gate_x_b[i], a_gate_a_w[i], a_gate_a_b[i], a_lambda[i],
                          b_group_w[i], b_group_b[i], b_scale[i], ab_w_out[i], tt=tt)
        else:
            j = layer // 2
            x = _conformer(x, c_norm[j], c_w_pw1[j], c_b_pw1[j], c_dw_w[j], c_dw_b[j],
                           c_ln_g[j], c_ln_b[j], c_w_pw2[j], c_b_pw2[j], tt=tt)
        k, v = _mem_kv(mem, xa_mem_norm[layer], xa_wk[layer], xa_wv[layer])
        x = _cross_attn(x, k, v, xa_norm[layer], xa_wq[layer], xa_wo[layer],
                        n_heads=XA_HEADS, ts=512)
        x = _conv_ffn(x, f_norm[layer], f_w_up[layer], f_dw_w[layer], f_dw_b[layer],
                      f_w_down[layer], final_norm, tt=tt, final_norm=(layer == depth - 1))
    return x
```

```python
import functools
import math

import jax
import jax.numpy as jnp
from jax import lax
from jax.experimental import pallas as pl
from jax.experimental.pallas import tpu as pltpu

EPS = 1e-6
C_RG = 8.0
POOL_WINDOWS = (2, 4, 8, 16)
XA_HEADS = 4
SUBLANES = 8
LANES = 128
VMEM_LIMIT_BYTES = 56 * 1024 * 1024

_BF = jnp.bfloat16
_F32 = jnp.float32


def _dot(a, b):
    return jnp.dot(a, b, preferred_element_type=_F32)


def _rms(x, g):
    return x * lax.rsqrt(jnp.mean(x * x, axis=-1, keepdims=True) + EPS) * g


def _sigmoid(x):
    return 1.0 / (1.0 + jnp.exp(-x))


def _gelu_tanh(x):
    c = math.sqrt(2.0 / math.pi)
    return 0.5 * x * (1.0 + jnp.tanh(c * (x + 0.044715 * (x * x * x))))


def _softplus(x):
    return jnp.maximum(x, 0.0) + jnp.log1p(jnp.exp(-jnp.abs(x)))


def _load_time_major(x_ref):
    tt, b, d = x_ref.shape
    return x_ref[...].reshape(tt * b, d)


def _store_time_major(o_ref, v):
    o_ref[...] = v.reshape(o_ref.shape)


def _const_spec(shape):
    nd = len(shape)
    return pl.BlockSpec(shape, lambda *_: (0,) * nd, pipeline_mode=pl.Buffered(1))


def _params(n_grid):
    return pltpu.CompilerParams(
        dimension_semantics=("arbitrary",) * n_grid,
        vmem_limit_bytes=VMEM_LIMIT_BYTES,
    )


def _row(v):
    return v.reshape(1, -1)


def _time_tiled_call(body, x, consts, scratch, *, tt, name):
    batch, seq, d = x.shape
    n_steps = seq // tt
    assert batch == SUBLANES and seq % tt == 0 and n_steps >= 2
    n_c, n_s = len(consts), len(scratch)

    def kern(*refs):
        x_hbm, c_refs, o_hbm = refs[0], refs[1:1 + n_c], refs[1 + n_c]
        s_refs = refs[2 + n_c:2 + n_c + n_s]
        xbuf, obuf, sem_in, sem_out = refs[2 + n_c + n_s:]
        i = pl.program_id(0)
        slot = lax.rem(i, 2)

        def copies_in(step, sl):
            return [pltpu.make_async_copy(x_hbm.at[b, pl.ds(step * tt, tt), :],
                                          xbuf.at[sl, :, b, :], sem_in.at[sl]) for b in range(batch)]

        def copies_out(step, sl):
            return [pltpu.make_async_copy(obuf.at[sl, :, b, :],
                                          o_hbm.at[b, pl.ds(step * tt, tt), :], sem_out.at[sl])
                    for b in range(batch)]

        @pl.when(i == 0)
        def _():
            for c in copies_in(0, 0):
                c.start()

        @pl.when(i + 1 < n_steps)
        def _():
            for c in copies_in(i + 1, 1 - slot):
                c.start()

        for c in copies_in(i, slot):
            c.wait()

        @pl.when(i >= 2)
        def _():
            for c in copies_out(i - 2, slot):
                c.wait()

        body(xbuf.at[slot], *c_refs, obuf.at[slot], *s_refs)

        for c in copies_out(i, slot):
            c.start()

        @pl.when(i == n_steps - 1)
        def _():
            for c in copies_out(i - 1, 1 - slot) + copies_out(i, slot):
                c.wait()

    any_spec = pl.BlockSpec(memory_space=pl.ANY)
    return pl.pallas_call(
        kern,
        grid=(n_steps,),
        in_specs=[any_spec] + [_const_spec(a.shape) for a in consts],
        out_specs=any_spec,
        out_shape=jax.ShapeDtypeStruct(x.shape, _F32),
        scratch_shapes=list(scratch) + [
            pltpu.VMEM((2, tt, batch, d), _F32),
            pltpu.VMEM((2, tt, batch, d), _F32),
            pltpu.SemaphoreType.DMA((2,)),
            pltpu.SemaphoreType.DMA((2,)),
        ],
        compiler_params=_params(1),
        name=name,
    )(x, *consts)


def _ab_kernel(x_ref, g_ref, win_ref, cw_ref, cb_ref, wg_ref, bgx_ref, bga_ref, lam_ref,
               wp_ref, bp_ref, sc_ref, wout_ref, o_ref,
               zrec_buf, pool_buf, a_buf, bx_buf, h_buf, h_carry, *, wa, wb, n_heads, cb_w):
    i = pl.program_id(0)
    tt = x_ref.shape[0]
    rows = tt * SUBLANES
    conv_k = cw_ref.shape[0]
    halo_c = (conv_k - 1) * SUBLANES
    halo_p = POOL_WINDOWS[-1] * SUBLANES
    hd = wa // n_heads
    hdb = wb // len(POOL_WINDOWS)

    @pl.when(i == 0)
    def _():
        zrec_buf[0:halo_c, :] = jnp.zeros((halo_c, wa), _F32)
        pool_buf[0:halo_p, :] = jnp.zeros((halo_p, wb), _F32)
        h_carry[...] = jnp.zeros_like(h_carry)

    x = _load_time_major(x_ref)
    xn = _rms(x, g_ref[...]).astype(_BF)
    sp = _softplus(-lam_ref[...])

    def in_proj(p):
        return (_dot(xn, win_ref[:, wa + p * cb_w:wa + (p + 1) * cb_w]),
                _dot(xn, win_ref[:, p * cb_w:(p + 1) * cb_w]))

    def pooled(z_pool):
        pool_buf[halo_p:halo_p + rows, :] = z_pool
        t_idx = i * tt + lax.broadcasted_iota(jnp.int32, (rows, 1), 0) // SUBLANES
        yb = []
        for g, w in enumerate(POOL_WINDOWS):
            gs = slice(g * hdb, (g + 1) * hdb)
            cur = pool_buf[:, gs]
            shift = 1
            while shift < w:
                cur = cur[shift * SUBLANES:, :] + cur[:cur.shape[0] - shift * SUBLANES, :]
                shift *= 2
            win_sum = cur[cur.shape[0] - rows:, :]
            cnt = jnp.minimum(t_idx + 1, w).astype(_F32)
            pm = (win_sum / cnt - z_pool[:, gs]).astype(_BF)
            yb.append((_dot(pm, wp_ref[g]) + bp_ref[:, gs]) * sc_ref[:, gs])
        pool_buf[0:halo_p, :] = pool_buf[rows:rows + halo_p, :]
        return jnp.concatenate(yb, axis=1).astype(_BF)

    out = x
    nxt = in_proj(0)
    n_blocks = wa // cb_w
    for p in range(n_blocks):
        cs = slice(p * cb_w, (p + 1) * cb_w)
        z_rec, z_gate = nxt

        zrec_buf[halo_c:halo_c + rows, cs] = z_rec
        xr = cb_ref[:, cs] + cw_ref[0:1, cs] * zrec_buf[0:rows, cs]
        for k in range(1, conv_k):
            xr = xr + cw_ref[k:k + 1, cs] * zrec_buf[k * SUBLANES:k * SUBLANES + rows, cs]
        zrec_buf[0:halo_c, cs] = zrec_buf[rows:rows + halo_c, cs]

        xr_b = xr.astype(_BF)
        heads = range(p * (cb_w // hd), (p + 1) * (cb_w // hd))
        gates = [_dot(xr_b[:, (h_idx * hd) % cb_w:(h_idx * hd) % cb_w + hd], wg_ref[h_idx])
                 for h_idx in heads]
        if p + 1 < n_blocks:
            nxt = in_proj(p + 1)
        for h_idx, gg in zip(heads, gates):
            hs = slice(h_idx * hd, (h_idx + 1) * hd)
            ls = slice((h_idx * hd) % cb_w, (h_idx * hd) % cb_w + hd)
            gate_x = _sigmoid(gg[:, :hd] + bgx_ref[:, hs])
            gate_a = _sigmoid(gg[:, hd:] + bga_ref[:, hs])
            a = jnp.exp((-C_RG) * gate_a * sp[:, hs])
            a_buf[:, hs] = a
            bx_buf[:, hs] = jnp.sqrt(1.0 - a * a) * (gate_x * xr[:, ls])

        h = h_carry[:, cs]
        for t in range(tt):
            ts = slice(t * SUBLANES, (t + 1) * SUBLANES)
            h = a_buf[ts, cs] * h + bx_buf[ts, cs]
            h_buf[ts, cs] = h
        h_carry[:, cs] = h

        y_a = (_gelu_tanh(z_gate) * h_buf[:, cs]).astype(_BF)
        out = out + _dot(y_a, wout_ref[cs, :])
        if p == 0:
            z_pool = _dot(xn, win_ref[:, 2 * wa:2 * wa + wb])
        elif p == 1:
            y_b = pooled(z_pool)
        elif p == 2:
            out = out + _dot(y_b, wout_ref[wa:wa + wb, :])

    _store_time_major(o_ref, out)


def _mixer_ab(x, norm, w_in, conv_w, conv_b, w_gx, b_gx, w_ga, b_ga, lam, w_pool, b_pool,
              pool_scale, w_out, *, tt):
    d = x.shape[2]
    n_heads, hd, _ = w_gx.shape
    wa = n_heads * hd
    wb = w_pool.shape[0] * w_pool.shape[1]
    rows = tt * SUBLANES
    halo_c = (conv_w.shape[0] - 1) * SUBLANES
    halo_p = POOL_WINDOWS[-1] * SUBLANES
    assert rows >= halo_p and w_in.shape == (d, 2 * wa + wb)
    wg = jnp.concatenate([w_gx, w_ga], axis=-1).astype(_BF)
    consts = (_row(norm), w_in.astype(_BF), conv_w, _row(conv_b), wg, _row(b_gx), _row(b_ga),
              _row(lam), w_pool.astype(_BF), _row(b_pool), _row(pool_scale), w_out.astype(_BF))
    scratch = [
        pltpu.VMEM((halo_c + rows, wa), _F32),
        pltpu.VMEM((halo_p + rows, wb), _F32),
        pltpu.VMEM((rows, wa), _F32),
        pltpu.VMEM((rows, wa), _F32),
        pltpu.VMEM((rows, wa), _F32),
        pltpu.VMEM((SUBLANES, wa), _F32),
    ]
    body = functools.partial(_ab_kernel, wa=wa, wb=wb, n_heads=n_heads, cb_w=2 * hd)
    return _time_tiled_call(body, x, consts, scratch, tt=tt, name="mixer_ab")


def _conf_kernel(x_ref, g_ref, w1_ref, b1_ref, dw_ref, db_ref, lg_ref, lb_ref, w2_ref, b2_ref,
                 o_ref, glu_buf, conv_buf, *, chunk):
    i = pl.program_id(0)
    tt, d = x_ref.shape[0], x_ref.shape[2]
    rows = tt * SUBLANES
    conv_k = dw_ref.shape[0]
    halo = (conv_k - 1) * SUBLANES

    @pl.when(i == 0)
    def _():
        glu_buf[0:halo, 0:d] = jnp.zeros((halo, d), _F32)

    x = _load_time_major(x_ref)
    xn = _rms(x, g_ref[...]).astype(_BF)
    ha = _dot(xn, w1_ref[:, 0:d]) + b1_ref[:, 0:d]
    hb = _dot(xn, w1_ref[:, d:2 * d]) + b1_ref[:, d:2 * d]
    glu_buf[halo:halo + rows, 0:d] = ha * _sigmoid(hb)

    def conv_chunk(c, carry):
        r0 = pl.multiple_of(c * chunk, chunk)
        for j in range(d // LANES):
            ls = slice(j * LANES, (j + 1) * LANES)
            acc = db_ref[:, ls] + dw_ref[0:1, ls] * glu_buf[pl.ds(r0, chunk), ls]
            for k in range(1, conv_k):
                acc = acc + dw_ref[k:k + 1, ls] * glu_buf[pl.ds(r0 + k * SUBLANES, chunk), ls]
            conv_buf[pl.ds(r0, chunk), ls] = acc
        return carry

    lax.fori_loop(0, rows // chunk, conv_chunk, 0)
    glu_buf[0:halo, 0:d] = glu_buf[rows:rows + halo, 0:d]

    c = conv_buf[...]
    mu = jnp.mean(c, axis=-1, keepdims=True)
    cc = c - mu
    var = jnp.mean(cc * cc, axis=-1, keepdims=True)
    y = cc * lax.rsqrt(var + EPS) * lg_ref[...] + lb_ref[...]
    y = (y * _sigmoid(y)).astype(_BF)
    _store_time_major(o_ref, x + _dot(y, w2_ref[...]) + b2_ref[...])


def _conformer(x, norm, w1, b1, dw_w, dw_b, ln_g, ln_b, w2, b2, *, tt, chunk=64):
    d = x.shape[2]
    rows = tt * SUBLANES
    halo = (dw_w.shape[0] - 1) * SUBLANES
    assert rows >= halo and rows % chunk == 0
    consts = (_row(norm), w1.astype(_BF), _row(b1), dw_w, _row(dw_b), _row(ln_g), _row(ln_b),
              w2.astype(_BF), _row(b2))
    scratch = [pltpu.VMEM((halo + rows, d + LANES), _F32), pltpu.VMEM((rows, d), _F32)]
    body = functools.partial(_conf_kernel, chunk=chunk)
    return _time_tiled_call(body, x, consts, scratch, tt=tt, name="conformer")


def _kv_kernel(m_ref, g_ref, wk_ref, wv_ref, k_ref, v_ref):
    m = _rms(m_ref[0], g_ref[...]).astype(_BF)
    k_ref[0] = _dot(m, wk_ref[...]).astype(_BF)
    v_ref[0] = _dot(m, wv_ref[...]).astype(_BF)


def _mem_kv(mem, mem_norm, wk, wv):
    b, m, d = mem.shape
    blk = pl.BlockSpec((1, m, d), lambda i: (i, 0, 0))
    return pl.pallas_call(
        _kv_kernel,
        grid=(b,),
        in_specs=[blk, _const_spec((1, d)), _const_spec((d, d)), _const_spec((d, d))],
        out_specs=[blk, blk],
        out_shape=[jax.ShapeDtypeStruct((b, m, d), _BF)] * 2,
        compiler_params=_params(1),
        name="mem_kv",
    )(mem, _row(mem_norm), wk.astype(_BF), wv.astype(_BF))


def _xa_kernel(x_ref, g_ref, wq_ref, k_ref, v_ref, wo_ref, o_ref, *, n_heads):
    d = x_ref.shape[2]
    hd = d // n_heads
    x = x_ref[0]
    xn = _rms(x, g_ref[...]).astype(_BF)
    q = (_dot(xn, wq_ref[...]) * (hd ** -0.5)).astype(_BF)
    outs = []
    for h in range(n_heads):
        hs = slice(h * hd, (h + 1) * hd)
        s = lax.dot_general(q[:, hs], k_ref[0, :, hs], (((1,), (1,)), ((), ())),
                            preferred_element_type=_F32)
        e = jnp.exp(s - jnp.max(s, axis=-1, keepdims=True))
        inv = 1.0 / jnp.sum(e, axis=-1, keepdims=True)
        outs.append(_dot(e.astype(_BF), v_ref[0, :, hs]) * inv)
    o = jnp.concatenate(outs, axis=1).astype(_BF)
    o_ref[0] = x + _dot(o, wo_ref[...])


def _cross_attn(x, k, v, norm, wq, wo, *, n_heads, ts):
    batch, seq, d = x.shape
    hd = d // n_heads
    assert seq % ts == 0 and math.log2(hd) % 2 == 0
    m = k.shape[1]
    tile = pl.BlockSpec((1, ts, d), lambda b, i: (b, i, 0))
    kv = pl.BlockSpec((1, m, d), lambda b, i: (b, 0, 0))
    return pl.pallas_call(
        functools.partial(_xa_kernel, n_heads=n_heads),
        grid=(batch, seq // ts),
        in_specs=[tile, _const_spec((1, d)), _const_spec((d, d)), kv, kv, _const_spec((d, d))],
        out_specs=tile,
        out_shape=jax.ShapeDtypeStruct(x.shape, _F32),
        compiler_params=_params(2),
        name="cross_attn",
    )(x, _row(norm), wq.astype(_BF), k, v, wo.astype(_BF))


def _ffn_kernel(x_ref, g_ref, wup_ref, dw_ref, db_ref, wdn_ref, fin_ref, o_ref, g_halo,
                *, fc, final_norm):
    i = pl.program_id(0)
    tt = x_ref.shape[0]
    rows = tt * SUBLANES
    d_ff = wdn_ref.shape[0]
    conv_k = dw_ref.shape[0]
    halo = (conv_k - 1) * SUBLANES

    @pl.when(i == 0)
    def _():
        g_halo[...] = jnp.zeros_like(g_halo)

    x = _load_time_major(x_ref)
    xn = _rms(x, g_ref[...]).astype(_BF)

    def up(c):
        return (_dot(xn, wup_ref[:, c * fc:(c + 1) * fc]),
                _dot(xn, wup_ref[:, d_ff + c * fc:d_ff + (c + 1) * fc]))

    acc = x
    n_chunks = d_ff // fc
    nxt = up(0)
    for c in range(n_chunks):
        cs = slice(c * fc, (c + 1) * fc)
        g, u = nxt
        if c + 1 < n_chunks:
            nxt = up(c + 1)
        g_ext = jnp.concatenate([g_halo[:, cs], g], axis=0)
        g_halo[:, cs] = g[rows - halo:, :]
        conv = db_ref[:, cs] + dw_ref[0:1, cs] * g_ext[0:rows, :]
        for k in range(1, conv_k):
            conv = conv + dw_ref[k:k + 1, cs] * g_ext[k * SUBLANES:k * SUBLANES + rows, :]
        act = (_gelu_tanh(conv) * u).astype(_BF)
        acc = acc + _dot(act, wdn_ref[cs, :])
    if final_norm:
        acc = _rms(acc, fin_ref[...])
    _store_time_major(o_ref, acc)


def _conv_ffn(x, norm, w_up, dw_w, dw_b, w_down, fin, *, tt, fc=512, final_norm=False):
    d_ff = w_down.shape[0]
    rows = tt * SUBLANES
    halo = (dw_w.shape[0] - 1) * SUBLANES
    assert d_ff % fc == 0 and rows >= halo
    consts = (_row(norm), w_up.astype(_BF), dw_w, _row(dw_b), w_down.astype(_BF), _row(fin))
    scratch = [pltpu.VMEM((halo, d_ff), _F32)]
    body = functools.partial(_ffn_kernel, fc=fc, final_norm=final_norm)
    return _time_tiled_call(body, x, consts, scratch, tt=tt, name="conv_ffn")


def kernel(x, mem, ab_norm, ab_w_in, a_conv_w, a_conv_b, a_gate_x_w, a_gate_x_b, a_gate_a_w, a_gate_a_b, a_lambda, b_group_w, b_group_b, b_scale, ab_w_out, c_norm, c_w_pw1, c_b_pw1, c_dw_w, c_dw_b, c_ln_g, c_ln_b, c_w_pw2, c_b_pw2, xa_norm, xa_mem_norm, xa_wq, xa_wk, xa_wv, xa_wo, f_norm, f_w_up, f_dw_w, f_dw_b, f_w_down, final_norm):
    depth = xa_norm.shape[0]
    tt = 64
    for layer in range(depth):
        if layer % 2 == 0:
            i = layer // 2
            x = _mixer_ab(x, ab_norm[i], ab_w_in[i], a_conv_w[i], a_conv_b[i], a_gate_x_w[i],
                          a_gate_x_b[i], a_gate_a_w[i], a_gate_a_b[i], a_lambda[i],
                          b_group_w[i], b_group_b[i], b_scale[i], ab_w_out[i], tt=tt)
        else:
            j = layer // 2
            x = _conformer(x, c_norm[j], c_w_pw1[j], c_b_pw1[j], c_dw_w[j], c_dw_b[j],
                           c_ln_g[j], c_ln_b[j], c_w_pw2[j], c_b_pw2[j], tt=tt)
        k, v = _mem_kv(mem, xa_mem_norm[layer], xa_wk[layer], xa_wv[layer])
        x = _cross_attn(x, k, v, xa_norm[layer], xa_wq[layer], xa_wo[layer],
                        n_heads=XA_HEADS, ts=1024)
        x = _conv_ffn(x, f_norm[layer], f_w_up[layer], f_dw_w[layer], f_dw_b[layer],
                      f_w_down[layer], final_norm, tt=tt, final_norm=(layer == depth - 1))
    return x
```

```python
import functools
import math

import jax
import jax.numpy as jnp
from jax import lax
from jax.experimental import pallas as pl
from jax.experimental.pallas import tpu as pltpu

EPS = 1e-6
C_RG = 8.0
POOL_WINDOWS = (2, 4, 8, 16)
XA_HEADS = 4
SUBLANES = 8
LANES = 128
VMEM_LIMIT_BYTES = 56 * 1024 * 1024

_BF = jnp.bfloat16
_F32 = jnp.float32


def _dot(a, b):
    return jnp.dot(a, b, preferred_element_type=_F32)


def _rms(x, g):
    return x * lax.rsqrt(jnp.mean(x * x, axis=-1, keepdims=True) + EPS) * g


def _sigmoid(x):
    return 1.0 / (1.0 + jnp.exp(-x))


def _gelu_tanh(x):
    c = math.sqrt(2.0 / math.pi)
    return 0.5 * x * (1.0 + jnp.tanh(c * (x + 0.044715 * (x * x * x))))


def _softplus(x):
    return jnp.maximum(x, 0.0) + jnp.log1p(jnp.exp(-jnp.abs(x)))


def _load_time_major(x_ref):
    tt, b, d = x_ref.shape
    return x_ref[...].reshape(tt * b, d)


def _store_time_major(o_ref, v):
    o_ref[...] = v.reshape(o_ref.shape)


def _const_spec(shape):
    nd = len(shape)
    return pl.BlockSpec(shape, lambda *_: (0,) * nd, pipeline_mode=pl.Buffered(1))


def _params(n_grid):
    return pltpu.CompilerParams(
        dimension_semantics=("arbitrary",) * n_grid,
        vmem_limit_bytes=VMEM_LIMIT_BYTES,
    )


def _row(v):
    return v.reshape(1, -1)


def _time_tiled_call(body, x, consts, scratch, *, tt, name):
    batch, seq, d = x.shape
    n_steps = seq // tt
    assert batch == SUBLANES and seq % tt == 0 and n_steps >= 2
    n_c, n_s = len(consts), len(scratch)

    def kern(*refs):
        x_hbm, c_refs, o_hbm = refs[0], refs[1:1 + n_c], refs[1 + n_c]
        s_refs = refs[2 + n_c:2 + n_c + n_s]
        xbuf, obuf, sem_in, sem_out = refs[2 + n_c + n_s:]
        i = pl.program_id(0)
        slot = lax.rem(i, 2)

        def copies_in(step, sl):
            return [pltpu.make_async_copy(x_hbm.at[b, pl.ds(step * tt, tt), :],
                                          xbuf.at[sl, :, b, :], sem_in.at[sl]) for b in range(batch)]

        def copies_out(step, sl):
            return [pltpu.make_async_copy(obuf.at[sl, :, b, :],
                                          o_hbm.at[b, pl.ds(step * tt, tt), :], sem_out.at[sl])
                    for b in range(batch)]

        @pl.when(i == 0)
        def _():
            for c in copies_in(0, 0):
                c.start()

        @pl.when(i + 1 < n_steps)
        def _():
            for c in copies_in(i + 1, 1 - slot):
                c.start()

        for c in copies_in(i, slot):
            c.wait()

        @pl.when(i >= 2)
        def _():
            for c in copies_out(i - 2, slot):
                c.wait()

        body(xbuf.at[slot], *c_refs, obuf.at[slot], *s_refs)

        for c in copies_out(i, slot):
            c.start()

        @pl.when(i == n_steps - 1)
        def _():
            for c in copies_out(i - 1, 1 - slot) + copies_out(i, slot):
                c.wait()

    any_spec = pl.BlockSpec(memory_space=pl.ANY)
    return pl.pallas_call(
        kern,
        grid=(n_steps,),
        in_specs=[any_spec] + [_const_spec(a.shape) for a in consts],
        out_specs=any_spec,
        out_shape=jax.ShapeDtypeStruct(x.shape, _F32),
        scratch_shapes=list(scratch) + [
            pltpu.VMEM((2, tt, batch, d), _F32),
            pltpu.VMEM((2, tt, batch, d), _F32),
            pltpu.SemaphoreType.DMA((2,)),
            pltpu.SemaphoreType.DMA((2,)),
        ],
        compiler_params=_params(1),
        name=name,
    )(x, *consts)


def _ab_kernel(x_ref, g_ref, win_ref, cw_ref, cb_ref, wg_ref, bgx_ref, bga_ref, lam_ref,
               wp_ref, bp_ref, sc_ref, wout_ref, o_ref,
               zrec_buf, pool_buf, a_buf, bx_buf, h_buf, h_carry, *, wa, wb, n_heads, cb_w):
    i = pl.program_id(0)
    tt = x_ref.shape[0]
    rows = tt * SUBLANES
    conv_k = cw_ref.shape[0]
    halo_c = (conv_k - 1) * SUBLANES
    halo_p = POOL_WINDOWS[-1] * SUBLANES
    hd = wa // n_heads
    hdb = wb // len(POOL_WINDOWS)

    @pl.when(i == 0)
    def _():
        zrec_buf[0:halo_c, :] = jnp.zeros((halo_c, wa), _F32)
        pool_buf[0:halo_p, :] = jnp.zeros((halo_p, wb), _F32)
        h_carry[...] = jnp.zeros_like(h_carry)

    x = _load_time_major(x_ref)
    xn = _rms(x, g_ref[...]).astype(_BF)
    sp = _softplus(-lam_ref[...])

    def in_proj(p):
        return (_dot(xn, win_ref[:, wa + p * cb_w:wa + (p + 1) * cb_w]),
                _dot(xn, win_ref[:, p * cb_w:(p + 1) * cb_w]))

    def pooled(z_pool):
        pool_buf[halo_p:halo_p + rows, :] = z_pool
        t_idx = i * tt + lax.broadcasted_iota(jnp.int32, (rows, 1), 0) // SUBLANES
        yb = []
        for g, w in enumerate(POOL_WINDOWS):
            gs = slice(g * hdb, (g + 1) * hdb)
            cur = pool_buf[:, gs]
            shift = 1
            while shift < w:
                cur = cur[shift * SUBLANES:, :] + cur[:cur.shape[0] - shift * SUBLANES, :]
                shift *= 2
            win_sum = cur[cur.shape[0] - rows:, :]
            cnt = jnp.minimum(t_idx + 1, w).astype(_F32)
            pm = (win_sum / cnt - z_pool[:, gs]).astype(_BF)
            yb.append((_dot(pm, wp_ref[g]) + bp_ref[:, gs]) * sc_ref[:, gs])
        pool_buf[0:halo_p, :] = pool_buf[rows:rows + halo_p, :]
        return jnp.concatenate(yb, axis=1).astype(_BF)

    out = x
    nxt = in_proj(0)
    n_blocks = wa // cb_w
    for p in range(n_blocks):
        cs = slice(p * cb_w, (p + 1) * cb_w)
        z_rec, z_gate = nxt

        zrec_buf[halo_c:halo_c + rows, cs] = z_rec
        xr = cb_ref[:, cs] + cw_ref[0:1, cs] * zrec_buf[0:rows, cs]
        for k in range(1, conv_k):
            xr = xr + cw_ref[k:k + 1, cs] * zrec_buf[k * SUBLANES:k * SUBLANES + rows, cs]
        zrec_buf[0:halo_c, cs] = zrec_buf[rows:rows + halo_c, cs]

        xr_b = xr.astype(_BF)
        heads = range(p * (cb_w // hd), (p + 1) * (cb_w // hd))
        gates = [_dot(xr_b[:, (h_idx * hd) % cb_w:(h_idx * hd) % cb_w + hd], wg_ref[h_idx])
                 for h_idx in heads]
        if p + 1 < n_blocks:
            nxt = in_proj(p + 1)
        for h_idx, gg in zip(heads, gates):
            hs = slice(h_idx * hd, (h_idx + 1) * hd)
            ls = slice((h_idx * hd) % cb_w, (h_idx * hd) % cb_w + hd)
            gate_x = _sigmoid(gg[:, :hd] + bgx_ref[:, hs])
            gate_a = _sigmoid(gg[:, hd:] + bga_ref[:, hs])
            a = jnp.exp((-C_RG) * gate_a * sp[:, hs])
            a_buf[:, hs] = a
            bx_buf[:, hs] = jnp.sqrt(1.0 - a * a) * (gate_x * xr[:, ls])

        h = h_carry[:, cs]
        for t in range(tt):
            ts = slice(t * SUBLANES, (t + 1) * SUBLANES)
            h = a_buf[ts, cs] * h + bx_buf[ts, cs]
            h_buf[ts, cs] = h
        h_carry[:, cs] = h

        y_a = (_gelu_tanh(z_gate) * h_buf[:, cs]).astype(_BF)
        out = out + _dot(y_a, wout_ref[cs, :])
        if p == 0:
            z_pool = _dot(xn, win_ref[:, 2 * wa:2 * wa + wb])
        elif p == 1:
            y_b = pooled(z_pool)
        elif p == 2:
            out = out + _dot(y_b, wout_ref[wa:wa + wb, :])

    _store_time_major(o_ref, out)


def _mixer_ab(x, norm, w_in, conv_w, conv_b, w_gx, b_gx, w_ga, b_ga, lam, w_pool, b_pool,
              pool_scale, w_out, *, tt):
    d = x.shape[2]
    n_heads, hd, _ = w_gx.shape
    wa = n_heads * hd
    wb = w_pool.shape[0] * w_pool.shape[1]
    rows = tt * SUBLANES
    halo_c = (conv_w.shape[0] - 1) * SUBLANES
    halo_p = POOL_WINDOWS[-1] * SUBLANES
    assert rows >= halo_p and w_in.shape == (d, 2 * wa + wb)
    wg = jnp.concatenate([w_gx, w_ga], axis=-1).astype(_BF)
    consts = (_row(norm), w_in.astype(_BF), conv_w, _row(conv_b), wg, _row(b_gx), _row(b_ga),
              _row(lam), w_pool.astype(_BF), _row(b_pool), _row(pool_scale), w_out.astype(_BF))
    scratch = [
        pltpu.VMEM((halo_c + rows, wa), _F32),
        pltpu.VMEM((halo_p + rows, wb), _F32),
        pltpu.VMEM((rows, wa), _F32),
        pltpu.VMEM((rows, wa), _F32),
        pltpu.VMEM((rows, wa), _F32),
        pltpu.VMEM((SUBLANES, wa), _F32),
    ]
    body = functools.partial(_ab_kernel, wa=wa, wb=wb, n_heads=n_heads, cb_w=2 * hd)
    return _time_tiled_call(body, x, consts, scratch, tt=tt, name="mixer_ab")


def _conf_kernel(x_ref, g_ref, w1_ref, b1_ref, dw_ref, db_ref, lg_ref, lb_ref, w2_ref, b2_ref,
                 o_ref, glu_buf, conv_buf, *, chunk):
    i = pl.program_id(0)
    tt, d = x_ref.shape[0], x_ref.shape[2]
    rows = tt * SUBLANES
    conv_k = dw_ref.shape[0]
    halo = (conv_k - 1) * SUBLANES

    @pl.when(i == 0)
    def _():
        glu_buf[0:halo, 0:d] = jnp.zeros((halo, d), _F32)

    x = _load_time_major(x_ref)
    xn = _rms(x, g_ref[...]).astype(_BF)
    ha = _dot(xn, w1_ref[:, 0:d]) + b1_ref[:, 0:d]
    hb = _dot(xn, w1_ref[:, d:2 * d]) + b1_ref[:, d:2 * d]
    glu_buf[halo:halo + rows, 0:d] = ha * _sigmoid(hb)

    def conv_chunk(c, carry):
        r0 = pl.multiple_of(c * chunk, chunk)
        for j in range(d // LANES):
            ls = slice(j * LANES, (j + 1) * LANES)
            acc = db_ref[:, ls] + dw_ref[0:1, ls] * glu_buf[pl.ds(r0, chunk), ls]
            for k in range(1, conv_k):
                acc = acc + dw_ref[k:k + 1, ls] * glu_buf[pl.ds(r0 + k * SUBLANES, chunk), ls]
            conv_buf[pl.ds(r0, chunk), ls] = acc
        return carry

    lax.fori_loop(0, rows // chunk, conv_chunk, 0)
    glu_buf[0:halo, 0:d] = glu_buf[rows:rows + halo, 0:d]

    c = conv_buf[...]
    mu = jnp.mean(c, axis=-1, keepdims=True)
    cc = c - mu
    var = jnp.mean(cc * cc, axis=-1, keepdims=True)
    y = cc * lax.rsqrt(var + EPS) * lg_ref[...] + lb_ref[...]
    y = (y * _sigmoid(y)).astype(_BF)
    _store_time_major(o_ref, x + _dot(y, w2_ref[...]) + b2_ref[...])


def _conformer(x, norm, w1, b1, dw_w, dw_b, ln_g, ln_b, w2, b2, *, tt, chunk=64):
    d = x.shape[2]
    rows = tt * SUBLANES
    halo = (dw_w.shape[0] - 1) * SUBLANES
    assert rows >= halo and rows % chunk == 0
    consts = (_row(norm), w1.astype(_BF), _row(b1), dw_w, _row(dw_b), _row(ln_g), _row(ln_b),
              w2.astype(_BF), _row(b2))
    scratch = [pltpu.VMEM((halo + rows, d + LANES), _F32), pltpu.VMEM((rows, d), _F32)]
    body = functools.partial(_conf_kernel, chunk=chunk)
    return _time_tiled_call(body, x, consts, scratch, tt=tt, name="conformer")


def _kv_kernel(m_ref, g_ref, wk_ref, wv_ref, k_ref, v_ref):
    m = _rms(m_ref[0], g_ref[...]).astype(_BF)
    k_ref[0] = _dot(m, wk_ref[...]).astype(_BF)
    v_ref[0] = _dot(m, wv_ref[...]).astype(_BF)


def _mem_kv(mem, mem_norm, wk, wv):
    b, m, d = mem.shape
    blk = pl.BlockSpec((1, m, d), lambda i: (i, 0, 0))
    return pl.pallas_call(
        _kv_kernel,
        grid=(b,),
        in_specs=[blk, _const_spec((1, d)), _const_spec((d, d)), _const_spec((d, d))],
        out_specs=[blk, blk],
        out_shape=[jax.ShapeDtypeStruct((b, m, d), _BF)] * 2,
        compiler_params=_params(1),
        name="mem_kv",
    )(mem, _row(mem_norm), wk.astype(_BF), wv.astype(_BF))


def _xa_kernel(x_ref, g_ref, wq_ref, k_ref, v_ref, wo_ref, o_ref, *, n_heads):
    d = x_ref.shape[2]
    hd = d // n_heads
    x = x_ref[0]
    xn = _rms(x, g_ref[...]).astype(_BF)
    q = (_dot(xn, wq_ref[...]) * (hd ** -0.5)).astype(_BF)
    outs = []
    for h in range(n_heads):
        hs = slice(h * hd, (h + 1) * hd)
        s = lax.dot_general(q[:, hs], k_ref[0, :, hs], (((1,), (1,)), ((), ())),
                            preferred_element_type=_F32)
        e = jnp.exp(s - jnp.max(s, axis=-1, keepdims=True))
        inv = 1.0 / jnp.sum(e, axis=-1, keepdims=True)
        outs.append(_dot(e.astype(_BF), v_ref[0, :, hs]) * inv)
    o = jnp.concatenate(outs, axis=1).astype(_BF)
    o_ref[0] = x + _dot(o, wo_ref[...])


def _cross_attn(x, k, v, norm, wq, wo, *, n_heads, ts):
    batch, seq, d = x.shape
    hd = d // n_heads
    assert seq % ts == 0 and math.log2(hd) % 2 == 0
    m = k.shape[1]
    tile = pl.BlockSpec((1, ts, d), lambda b, i: (b, i, 0))
    kv = pl.BlockSpec((1, m, d), lambda b, i: (b, 0, 0))
    return pl.pallas_call(
        functools.partial(_xa_kernel, n_heads=n_heads),
        grid=(batch, seq // ts),
        in_specs=[tile, _const_spec((1, d)), _const_spec((d, d)), kv, kv, _const_spec((d, d))],
        out_specs=tile,
        out_shape=jax.ShapeDtypeStruct(x.shape, _F32),
        compiler_params=_params(2),
        name="cross_attn",
    )(x, _row(norm), wq.astype(_BF), k, v, wo.astype(_BF))


def _ffn_kernel(x_ref, g_ref, wup_ref, dw_ref, db_ref, wdn_ref, fin_ref, o_ref, g_halo,
                *, fc, final_norm):
    i = pl.program_id(0)
    tt = x_ref.shape[0]
    rows = tt * SUBLANES
    d_ff = wdn_ref.shape[0]
    conv_k = dw_ref.shape[0]
    halo = (conv_k - 1) * SUBLANES

    @pl.when(i == 0)
    def _():
        g_halo[...] = jnp.zeros_like(g_halo)

    x = _load_time_major(x_ref)
    xn = _rms(x, g_ref[...]).astype(_BF)

    def up(c):
        return (_dot(xn, wup_ref[:, c * fc:(c + 1) * fc]),
                _dot(xn, wup_ref[:, d_ff + c * fc:d_ff + (c + 1) * fc]))

    acc = x
    n_chunks = d_ff // fc
    nxt = up(0)
    for c in range(n_chunks):
        cs = slice(c * fc, (c + 1) * fc)
        g, u = nxt
        if c + 1 < n_chunks:
            nxt = up(c + 1)
        g_ext = jnp.concatenate([g_halo[:, cs], g], axis=0)
        g_halo[:, cs] = g[rows - halo:, :]
        conv = db_ref[:, cs] + dw_ref[0:1, cs] * g_ext[0:rows, :]
        for k in range(1, conv_k):
            conv = conv + dw_ref[k:k + 1, cs] * g_ext[k * SUBLANES:k * SUBLANES + rows, :]
        act = (_gelu_tanh(conv) * u).astype(_BF)
        acc = acc + _dot(act, wdn_ref[cs, :])
    if final_norm:
        acc = _rms(acc, fin_ref[...])
    _store_time_major(o_ref, acc)


def _conv_ffn(x, norm, w_up, dw_w, dw_b, w_down, fin, *, tt, fc=1024, final_norm=False):
    d_ff = w_down.shape[0]
    rows = tt * SUBLANES
    halo = (dw_w.shape[0] - 1) * SUBLANES
    assert d_ff % fc == 0 and rows >= halo
    consts = (_row(norm), w_up.astype(_BF), dw_w, _row(dw_b), w_down.astype(_BF), _row(fin))
    scratch = [pltpu.VMEM((halo, d_ff), _F32)]
    body = functools.partial(_ffn_kernel, fc=fc, final_norm=final_norm)
    return _time_tiled_call(body, x, consts, scratch, tt=tt, name="conv_ffn")


def kernel(x, mem, ab_norm, ab_w_in, a_conv_w, a_conv_b, a_gate_x_w, a_gate_x_b, a_gate_a_w, a_gate_a_b, a_lambda, b_group_w, b_group_b, b_scale, ab_w_out, c_norm, c_w_pw1, c_b_pw1, c_dw_w, c_dw_b, c_ln_g, c_ln_b, c_w_pw2, c_b_pw2, xa_norm, xa_mem_norm, xa_wq, xa_wk, xa_wv, xa_wo, f_norm, f_w_up, f_dw_w, f_dw_b, f_w_down, final_norm):
    depth = xa_norm.shape[0]
    tt = 64
    for layer in range(depth):
        if layer % 2 == 0:
            i = layer // 2
            x = _mixer_ab(x, ab_norm[i], ab_w_in[i], a_conv_w[i], a_conv_b[i], a_gate_x_w[i],
                          a_gate_x_b[i], a_gate_a_w[i], a_gate_a_b[i], a_lambda[i],
                          b_group_w[i], b_group_b[i], b_scale[i], ab_w_out[i], tt=tt)
        else:
            j = layer // 2
            x = _conformer(x, c_norm[j], c_w_pw1[j], c_b_pw1[j], c_dw_w[j], c_dw_b[j],
                           c_ln_g[j], c_ln_b[j], c_w_pw2[j], c_b_pw2[j], tt=tt)
        k, v = _mem_kv(mem, xa_mem_norm[layer], xa_wk[layer], xa_wv[layer])
        x = _cross_attn(x, k, v, xa_norm[layer], xa_wq[layer], xa_wo[layer],
                        n_heads=XA_HEADS, ts=1024)
        x = _conv_ffn(x, f_norm[layer], f_w_up[layer], f_dw_w[layer], f_dw_b[layer],
                      f_w_down[layer], final_norm, tt=tt, final_norm=(layer == depth - 1))
    return x
```

```python
import functools
import math

import jax
import jax.numpy as jnp
from jax import lax
from jax.experimental import pallas as pl
from jax.experimental.pallas import tpu as pltpu

EPS = 1e-6
C_RG = 8.0
POOL_WINDOWS = (2, 4, 8, 16)
XA_HEADS = 4
SUBLANES = 8
LANES = 128
VMEM_LIMIT_BYTES = 56 * 1024 * 1024

_BF = jnp.bfloat16
_F32 = jnp.float32


def _dot(a, b):
    return jnp.dot(a, b, preferred_element_type=_F32)


def _rms(x, g):
    return x * lax.rsqrt(jnp.mean(x * x, axis=-1, keepdims=True) + EPS) * g


def _sigmoid(x):
    return 1.0 / (1.0 + jnp.exp(-x))


def _gelu_tanh(x):
    c = math.sqrt(2.0 / math.pi)
    return 0.5 * x * (1.0 + jnp.tanh(c * (x + 0.044715 * (x * x * x))))


def _softplus(x):
    return jnp.maximum(x, 0.0) + jnp.log1p(jnp.exp(-jnp.abs(x)))


def _load_time_major(x_ref):
    tt, b, d = x_ref.shape
    return x_ref[...].reshape(tt * b, d)


def _store_time_major(o_ref, v):
    o_ref[...] = v.reshape(o_ref.shape)


def _const_spec(shape):
    nd = len(shape)
    return pl.BlockSpec(shape, lambda *_: (0,) * nd, pipeline_mode=pl.Buffered(1))


def _params(n_grid):
    return pltpu.CompilerParams(
        dimension_semantics=("arbitrary",) * n_grid,
        vmem_limit_bytes=VMEM_LIMIT_BYTES,
    )


def _row(v):
    return v.reshape(1, -1)


def _time_tiled_call(body, x, consts, scratch, *, tt, name, lookahead=False):
    batch, seq, d = x.shape
    n_steps = seq // tt
    n_in = 3 if lookahead else 2
    assert batch == SUBLANES and seq % tt == 0 and n_steps >= n_in
    n_c, n_s = len(consts), len(scratch)

    def kern(*refs):
        x_hbm, c_refs, o_hbm = refs[0], refs[1:1 + n_c], refs[1 + n_c]
        s_refs = refs[2 + n_c:2 + n_c + n_s]
        xbuf, obuf, sem_in, sem_out = refs[2 + n_c + n_s:]
        i = pl.program_id(0)
        slot = lax.rem(i, 2)
        in_slot = lax.rem(i, n_in)
        next_slot = lax.rem(i + 1, n_in)
        fetch = i + n_in - 1

        def copies_in(step, sl):
            return [pltpu.make_async_copy(x_hbm.at[b, pl.ds(step * tt, tt), :],
                                          xbuf.at[sl, :, b, :], sem_in.at[sl]) for b in range(batch)]

        def copies_out(step, sl):
            return [pltpu.make_async_copy(obuf.at[sl, :, b, :],
                                          o_hbm.at[b, pl.ds(step * tt, tt), :], sem_out.at[sl])
                    for b in range(batch)]

        @pl.when(i == 0)
        def _():
            for s in range(n_in - 1):
                for c in copies_in(s, s):
                    c.start()

        @pl.when(fetch < n_steps)
        def _():
            for c in copies_in(fetch, lax.rem(fetch, n_in)):
                c.start()

        if lookahead:
            @pl.when(i == 0)
            def _():
                for c in copies_in(0, 0):
                    c.wait()

            @pl.when(i + 1 < n_steps)
            def _():
                for c in copies_in(i + 1, next_slot):
                    c.wait()
        else:
            for c in copies_in(i, in_slot):
                c.wait()

        @pl.when(i >= 2)
        def _():
            for c in copies_out(i - 2, slot):
                c.wait()

        x_tiles = (xbuf.at[in_slot], xbuf.at[next_slot]) if lookahead else (xbuf.at[in_slot],)
        body(*x_tiles, *c_refs, obuf.at[slot], *s_refs)

        for c in copies_out(i, slot):
            c.start()

        @pl.when(i == n_steps - 1)
        def _():
            for c in copies_out(i - 1, 1 - slot) + copies_out(i, slot):
                c.wait()

    any_spec = pl.BlockSpec(memory_space=pl.ANY)
    return pl.pallas_call(
        kern,
        grid=(n_steps,),
        in_specs=[any_spec] + [_const_spec(a.shape) for a in consts],
        out_specs=any_spec,
        out_shape=jax.ShapeDtypeStruct(x.shape, _F32),
        scratch_shapes=list(scratch) + [
            pltpu.VMEM((n_in, tt, batch, d), _F32),
            pltpu.VMEM((2, tt, batch, d), _F32),
            pltpu.SemaphoreType.DMA((n_in,)),
            pltpu.SemaphoreType.DMA((2,)),
        ],
        compiler_params=_params(1),
        name=name,
    )(x, *consts)


def _ab_kernel(x_ref, g_ref, win_ref, cw_ref, cb_ref, wg_ref, bgx_ref, bga_ref, lam_ref,
               wp_ref, bp_ref, sc_ref, wout_ref, o_ref,
               zrec_buf, pool_buf, a_buf, bx_buf, h_buf, h_carry, *, wa, wb, n_heads, cb_w):
    i = pl.program_id(0)
    tt = x_ref.shape[0]
    rows = tt * SUBLANES
    conv_k = cw_ref.shape[0]
    halo_c = (conv_k - 1) * SUBLANES
    halo_p = POOL_WINDOWS[-1] * SUBLANES
    hd = wa // n_heads
    hdb = wb // len(POOL_WINDOWS)

    @pl.when(i == 0)
    def _():
        zrec_buf[0:halo_c, :] = jnp.zeros((halo_c, wa), _F32)
        pool_buf[0:halo_p, :] = jnp.zeros((halo_p, wb), _F32)
        h_carry[...] = jnp.zeros_like(h_carry)

    x = _load_time_major(x_ref)
    xn = _rms(x, g_ref[...]).astype(_BF)
    sp = _softplus(-lam_ref[...])

    def in_proj(p):
        return (_dot(xn, win_ref[:, wa + p * cb_w:wa + (p + 1) * cb_w]),
                _dot(xn, win_ref[:, p * cb_w:(p + 1) * cb_w]))

    def pooled(z_pool):
        pool_buf[halo_p:halo_p + rows, :] = z_pool
        t_idx = i * tt + lax.broadcasted_iota(jnp.int32, (rows, 1), 0) // SUBLANES
        yb = []
        for g, w in enumerate(POOL_WINDOWS):
            gs = slice(g * hdb, (g + 1) * hdb)
            cur = pool_buf[:, gs]
            shift = 1
            while shift < w:
                cur = cur[shift * SUBLANES:, :] + cur[:cur.shape[0] - shift * SUBLANES, :]
                shift *= 2
            win_sum = cur[cur.shape[0] - rows:, :]
            cnt = jnp.minimum(t_idx + 1, w).astype(_F32)
            pm = (win_sum / cnt - z_pool[:, gs]).astype(_BF)
            yb.append((_dot(pm, wp_ref[g]) + bp_ref[:, gs]) * sc_ref[:, gs])
        pool_buf[0:halo_p, :] = pool_buf[rows:rows + halo_p, :]
        return jnp.concatenate(yb, axis=1).astype(_BF)

    out = x
    nxt = in_proj(0)
    n_blocks = wa // cb_w
    for p in range(n_blocks):
        cs = slice(p * cb_w, (p + 1) * cb_w)
        z_rec, z_gate = nxt

        zrec_buf[halo_c:halo_c + rows, cs] = z_rec
        xr = cb_ref[:, cs] + cw_ref[0:1, cs] * zrec_buf[0:rows, cs]
        for k in range(1, conv_k):
            xr = xr + cw_ref[k:k + 1, cs] * zrec_buf[k * SUBLANES:k * SUBLANES + rows, cs]
        zrec_buf[0:halo_c, cs] = zrec_buf[rows:rows + halo_c, cs]

        xr_b = xr.astype(_BF)
        heads = range(p * (cb_w // hd), (p + 1) * (cb_w // hd))
        gates = [_dot(xr_b[:, (h_idx * hd) % cb_w:(h_idx * hd) % cb_w + hd], wg_ref[h_idx])
                 for h_idx in heads]
        if p + 1 < n_blocks:
            nxt = in_proj(p + 1)
        for h_idx, gg in zip(heads, gates):
            hs = slice(h_idx * hd, (h_idx + 1) * hd)
            ls = slice((h_idx * hd) % cb_w, (h_idx * hd) % cb_w + hd)
            gate_x = _sigmoid(gg[:, :hd] + bgx_ref[:, hs])
            gate_a = _sigmoid(gg[:, hd:] + bga_ref[:, hs])
            a = jnp.exp((-C_RG) * gate_a * sp[:, hs])
            a_buf[:, hs] = a
            bx_buf[:, hs] = jnp.sqrt(1.0 - a * a) * (gate_x * xr[:, ls])

        h = h_carry[:, cs]
        for t in range(tt):
            ts = slice(t * SUBLANES, (t + 1) * SUBLANES)
            h = a_buf[ts, cs] * h + bx_buf[ts, cs]
            h_buf[ts, cs] = h
        h_carry[:, cs] = h

        y_a = (_gelu_tanh(z_gate) * h_buf[:, cs]).astype(_BF)
        out = out + _dot(y_a, wout_ref[cs, :])
        if p == 0:
            z_pool = _dot(xn, win_ref[:, 2 * wa:2 * wa + wb])
        elif p == 1:
            y_b = pooled(z_pool)
        elif p == 2:
            out = out + _dot(y_b, wout_ref[wa:wa + wb, :])

    _store_time_major(o_ref, out)


def _mixer_ab(x, norm, w_in, conv_w, conv_b, w_gx, b_gx, w_ga, b_ga, lam, w_pool, b_pool,
              pool_scale, w_out, *, tt):
    d = x.shape[2]
    n_heads, hd, _ = w_gx.shape
    wa = n_heads * hd
    wb = w_pool.shape[0] * w_pool.shape[1]
    rows = tt * SUBLANES
    halo_c = (conv_w.shape[0] - 1) * SUBLANES
    halo_p = POOL_WINDOWS[-1] * SUBLANES
    assert rows >= halo_p and w_in.shape == (d, 2 * wa + wb)
    wg = jnp.concatenate([w_gx, w_ga], axis=-1).astype(_BF)
    consts = (_row(norm), w_in.astype(_BF), conv_w, _row(conv_b), wg, _row(b_gx), _row(b_ga),
              _row(lam), w_pool.astype(_BF), _row(b_pool), _row(pool_scale), w_out.astype(_BF))
    scratch = [
        pltpu.VMEM((halo_c + rows, wa), _F32),
        pltpu.VMEM((halo_p + rows, wb), _F32),
        pltpu.VMEM((rows, wa), _F32),
        pltpu.VMEM((rows, wa), _F32),
        pltpu.VMEM((rows, wa), _F32),
        pltpu.VMEM((SUBLANES, wa), _F32),
    ]
    body = functools.partial(_ab_kernel, wa=wa, wb=wb, n_heads=n_heads, cb_w=2 * hd)
    return _time_tiled_call(body, x, consts, scratch, tt=tt, name="mixer_ab")


def _conf_kernel(x_ref, xnext_ref, g_ref, w1_ref, b1_ref, dw_ref, db_ref, lg_ref, lb_ref, w2_ref,
                 b2_ref, o_ref, glu_buf, conv_buf, h_pre, xn_next, *, chunk):
    i = pl.program_id(0)
    tt, d = x_ref.shape[0], x_ref.shape[2]
    rows = tt * SUBLANES
    conv_k = dw_ref.shape[0]
    halo = (conv_k - 1) * SUBLANES
    n_trips, _, lw = w1_ref.shape
    half = n_trips // 2
    trip_rows = rows // n_trips

    @pl.when(i == 0)
    def _():
        glu_buf[0:halo, 0:d] = jnp.zeros((halo, d), _F32)
        xn0 = _rms(_load_time_major(x_ref), g_ref[...]).astype(_BF)
        for s in range(n_trips):
            h_pre[s] = _dot(xn0, w1_ref[s])

    x = _load_time_major(x_ref)
    for j in range(half):
        ls = slice(j * lw, (j + 1) * lw)
        ha = h_pre[j] + b1_ref[:, ls]
        hb = h_pre[half + j] + b1_ref[:, d + j * lw:d + (j + 1) * lw]
        glu_buf[halo:halo + rows, ls] = ha * _sigmoid(hb)
    xn_next[...] = _rms(_load_time_major(xnext_ref), g_ref[...]).astype(_BF)

    def conv_trip(c, carry):
        h_pre[c] = _dot(xn_next[...], w1_ref[c])
        for s in range(trip_rows // chunk):
            r0 = pl.multiple_of(c * trip_rows + s * chunk, chunk)
            for j in range(d // LANES):
                ls = slice(j * LANES, (j + 1) * LANES)
                acc = db_ref[:, ls] + dw_ref[0:1, ls] * glu_buf[pl.ds(r0, chunk), ls]
                for k in range(1, conv_k):
                    acc = acc + dw_ref[k:k + 1, ls] * glu_buf[pl.ds(r0 + k * SUBLANES, chunk), ls]
                conv_buf[pl.ds(r0, chunk), ls] = acc
        return carry

    lax.fori_loop(0, n_trips, conv_trip, 0)
    glu_buf[0:halo, 0:d] = glu_buf[rows:rows + halo, 0:d]

    c = conv_buf[...]
    mu = jnp.mean(c, axis=-1, keepdims=True)
    cc = c - mu
    var = jnp.mean(cc * cc, axis=-1, keepdims=True)
    y = cc * lax.rsqrt(var + EPS) * lg_ref[...] + lb_ref[...]
    y = (y * _sigmoid(y)).astype(_BF)
    _store_time_major(o_ref, x + _dot(y, w2_ref[...]) + b2_ref[...])


def _conformer(x, norm, w1, b1, dw_w, dw_b, ln_g, ln_b, w2, b2, *, tt, chunk=64, n_trips=4):
    d = x.shape[2]
    rows = tt * SUBLANES
    halo = (dw_w.shape[0] - 1) * SUBLANES
    lw = 2 * d // n_trips
    assert rows >= halo and rows % (n_trips * chunk) == 0 and n_trips % 2 == 0 and lw % LANES == 0
    w1_slices = w1.astype(_BF).reshape(d, n_trips, lw).transpose(1, 0, 2)
    consts = (_row(norm), w1_slices, _row(b1), dw_w, _row(dw_b), _row(ln_g), _row(ln_b),
              w2.astype(_BF), _row(b2))
    scratch = [
        pltpu.VMEM((halo + rows, d + LANES), _F32),
        pltpu.VMEM((rows, d), _F32),
        pltpu.VMEM((n_trips, rows, lw), _F32),
        pltpu.VMEM((rows, d), _BF),
    ]
    body = functools.partial(_conf_kernel, chunk=chunk)
    return _time_tiled_call(body, x, consts, scratch, tt=tt, name="conformer", lookahead=True)


def _kv_kernel(m_ref, g_ref, wk_ref, wv_ref, k_ref, v_ref):
    m = _rms(m_ref[0], g_ref[...]).astype(_BF)
    k_ref[0] = _dot(m, wk_ref[...]).astype(_BF)
    v_ref[0] = _dot(m, wv_ref[...]).astype(_BF)


def _mem_kv(mem, mem_norm, wk, wv):
    b, m, d = mem.shape
    blk = pl.BlockSpec((1, m, d), lambda i: (i, 0, 0))
    return pl.pallas_call(
        _kv_kernel,
        grid=(b,),
        in_specs=[blk, _const_spec((1, d)), _const_spec((d, d)), _const_spec((d, d))],
        out_specs=[blk, blk],
        out_shape=[jax.ShapeDtypeStruct((b, m, d), _BF)] * 2,
        compiler_params=_params(1),
        name="mem_kv",
    )(mem, _row(mem_norm), wk.astype(_BF), wv.astype(_BF))


def _xa_kernel(x_ref, g_ref, wq_ref, k_ref, v_ref, wo_ref, o_ref, *, n_heads):
    d = x_ref.shape[2]
    hd = d // n_heads
    x = x_ref[0]
    xn = _rms(x, g_ref[...]).astype(_BF)
    q = (_dot(xn, wq_ref[...]) * (hd ** -0.5)).astype(_BF)
    outs = []
    for h in range(n_heads):
        hs = slice(h * hd, (h + 1) * hd)
        s = lax.dot_general(q[:, hs], k_ref[0, :, hs], (((1,), (1,)), ((), ())),
                            preferred_element_type=_F32)
        e = jnp.exp(s - jnp.max(s, axis=-1, keepdims=True))
        inv = 1.0 / jnp.sum(e, axis=-1, keepdims=True)
        outs.append(_dot(e.astype(_BF), v_ref[0, :, hs]) * inv)
    o = jnp.concatenate(outs, axis=1).astype(_BF)
    o_ref[0] = x + _dot(o, wo_ref[...])


def _cross_attn(x, k, v, norm, wq, wo, *, n_heads, ts):
    batch, seq, d = x.shape
    hd = d // n_heads
    assert seq % ts == 0 and math.log2(hd) % 2 == 0
    m = k.shape[1]
    tile = pl.BlockSpec((1, ts, d), lambda b, i: (b, i, 0))
    kv = pl.BlockSpec((1, m, d), lambda b, i: (b, 0, 0))
    return pl.pallas_call(
        functools.partial(_xa_kernel, n_heads=n_heads),
        grid=(batch, seq // ts),
        in_specs=[tile, _const_spec((1, d)), _const_spec((d, d)), kv, kv, _const_spec((d, d))],
        out_specs=tile,
        out_shape=jax.ShapeDtypeStruct(x.shape, _F32),
        compiler_params=_params(2),
        name="cross_attn",
    )(x, _row(norm), wq.astype(_BF), k, v, wo.astype(_BF))


def _ffn_kernel(x_ref, g_ref, wup_ref, dw_ref, db_ref, wdn_ref, fin_ref, o_ref, g_halo,
                *, fc, final_norm):
    i = pl.program_id(0)
    tt = x_ref.shape[0]
    rows = tt * SUBLANES
    d_ff = wdn_ref.shape[0]
    conv_k = dw_ref.shape[0]
    halo = (conv_k - 1) * SUBLANES

    @pl.when(i == 0)
    def _():
        g_halo[...] = jnp.zeros_like(g_halo)

    x = _load_time_major(x_ref)
    xn = _rms(x, g_ref[...]).astype(_BF)

    def up(c):
        return (_dot(xn, wup_ref[:, c * fc:(c + 1) * fc]),
                _dot(xn, wup_ref[:, d_ff + c * fc:d_ff + (c + 1) * fc]))

    acc = x
    n_chunks = d_ff // fc
    nxt = up(0)
    for c in range(n_chunks):
        cs = slice(c * fc, (c + 1) * fc)
        g, u = nxt
        if c + 1 < n_chunks:
            nxt = up(c + 1)
        g_ext = jnp.concatenate([g_halo[:, cs], g], axis=0)
        g_halo[:, cs] = g[rows - halo:, :]
        conv = db_ref[:, cs] + dw_ref[0:1, cs] * g_ext[0:rows, :]
        for k in range(1, conv_k):
            conv = conv + dw_ref[k:k + 1, cs] * g_ext[k * SUBLANES:k * SUBLANES + rows, :]
        act = (_gelu_tanh(conv) * u).astype(_BF)
        acc = acc + _dot(act, wdn_ref[cs, :])
    if final_norm:
        acc = _rms(acc, fin_ref[...])
    _store_time_major(o_ref, acc)


def _conv_ffn(x, norm, w_up, dw_w, dw_b, w_down, fin, *, tt, fc=1024, final_norm=False):
    d_ff = w_down.shape[0]
    rows = tt * SUBLANES
    halo = (dw_w.shape[0] - 1) * SUBLANES
    assert d_ff % fc == 0 and rows >= halo
    consts = (_row(norm), w_up.astype(_BF), dw_w, _row(dw_b), w_down.astype(_BF), _row(fin))
    scratch = [pltpu.VMEM((halo, d_ff), _F32)]
    body = functools.partial(_ffn_kernel, fc=fc, final_norm=final_norm)
    return _time_tiled_call(body, x, consts, scratch, tt=tt, name="conv_ffn")


def kernel(x, mem, ab_norm, ab_w_in, a_conv_w, a_conv_b, a_gate_x_w, a_gate_x_b, a_gate_a_w, a_gate_a_b, a_lambda, b_group_w, b_group_b, b_scale, ab_w_out, c_norm, c_w_pw1, c_b_pw1, c_dw_w, c_dw_b, c_ln_g, c_ln_b, c_w_pw2, c_b_pw2, xa_norm, xa_mem_norm, xa_wq, xa_wk, xa_wv, xa_wo, f_norm, f_w_up, f_dw_w, f_dw_b, f_w_down, final_norm):
    depth = xa_norm.shape[0]
    tt = 64
    for layer in range(depth):
        if layer % 2 == 0:
            i = layer // 2
            x = _mixer_ab(x, ab_norm[i], ab_w_in[i], a_conv_w[i], a_conv_b[i], a_gate_x_w[i],
                          a_gate_x_b[i], a_gate_a_w[i], a_gate_a_b[i], a_lambda[i],
                          b_group_w[i], b_group_b[i], b_scale[i], ab_w_out[i], tt=tt)
        else:
            j = layer // 2
            x = _conformer(x, c_norm[j], c_w_pw1[j], c_b_pw1[j], c_dw_w[j], c_dw_b[j],
                           c_ln_g[j], c_ln_b[j], c_w_pw2[j], c_b_pw2[j], tt=tt)
        k, v = _mem_kv(mem, xa_mem_norm[layer], xa_wk[layer], xa_wv[layer])
        x = _cross_attn(x, k, v, xa_norm[layer], xa_wq[layer], xa_wo[layer],
                        n_heads=XA_HEADS, ts=1024)
        x = _conv_ffn(x, f_norm[layer], f_w_up[layer], f_dw_w[layer], f_dw_b[layer],
                      f_w_down[layer], final_norm, tt=tt, final_norm=(layer == depth - 1))
    return x
```

```python
import functools
import math

import jax
import jax.numpy as jnp
from jax import lax
from jax.experimental import pallas as pl
from jax.experimental.pallas import tpu as pltpu

EPS = 1e-6
C_RG = 8.0
POOL_WINDOWS = (2, 4, 8, 16)
XA_HEADS = 4
SUBLANES = 8
LANES = 128
VMEM_LIMIT_BYTES = 56 * 1024 * 1024

_BF = jnp.bfloat16
_F32 = jnp.float32


def _dot(a, b):
    return jnp.dot(a, b, preferred_element_type=_F32)


def _rms(x, g):
    return x * lax.rsqrt(jnp.mean(x * x, axis=-1, keepdims=True) + EPS) * g


def _sigmoid(x):
    return 1.0 / (1.0 + jnp.exp(-x))


def _gelu_tanh(x):
    c = math.sqrt(2.0 / math.pi)
    return 0.5 * x * (1.0 + jnp.tanh(c * (x + 0.044715 * (x * x * x))))


def _softplus(x):
    return jnp.maximum(x, 0.0) + jnp.log1p(jnp.exp(-jnp.abs(x)))


def _load_time_major(x_ref):
    tt, b, d = x_ref.shape
    return x_ref[...].reshape(tt * b, d)


def _store_time_major(o_ref, v):
    o_ref[...] = v.reshape(o_ref.shape)


def _const_spec(shape):
    nd = len(shape)
    return pl.BlockSpec(shape, lambda *_: (0,) * nd, pipeline_mode=pl.Buffered(1))


def _params(n_grid):
    return pltpu.CompilerParams(
        dimension_semantics=("arbitrary",) * n_grid,
        vmem_limit_bytes=VMEM_LIMIT_BYTES,
    )


def _row(v):
    return v.reshape(1, -1)


STAGE_BYTES = 1 << 20


def _stage_rows(n_rows, n_cols):
    rc = 1 << int(math.log2(max(SUBLANES, STAGE_BYTES // (4 * n_cols))))
    while n_rows % rc:
        rc //= 2
    assert rc >= SUBLANES
    return rc


def _stage_to_bf16(w_hbm, dst, stage, sem):
    rc = stage.shape[1]
    copies = [pltpu.make_async_copy(w_hbm.at[pl.ds(k * rc, rc), :], stage.at[k % 2], sem.at[k % 2])
              for k in range(dst.shape[0] // rc)]
    copies[0].start()
    for k, c in enumerate(copies):
        if k + 1 < len(copies):
            copies[k + 1].start()
        c.wait()
        dst[k * rc:(k + 1) * rc, :] = stage[k % 2].astype(_BF)


def _weight_scratch(weights):
    out = []
    for w, _ in weights:
        _, r, c = w.shape
        out += [pltpu.VMEM((r, c), _BF), pltpu.VMEM((2, _stage_rows(r, c), c), _F32),
                pltpu.SemaphoreType.DMA((2,))]
    return out


def _stage_weights(weights, w_hbms, w_scratch):
    for k, (_, layer) in enumerate(weights):
        dst, stage, sem = w_scratch[3 * k:3 * k + 3]
        _stage_to_bf16(w_hbms[k].at[layer], dst, stage, sem)


def _time_tiled_call(body, x, consts, weights, scratch, *, tt, name):
    batch, seq, d = x.shape
    n_steps = seq // tt
    assert batch == SUBLANES and seq % tt == 0 and n_steps >= 2
    n_c, n_w, n_s = len(consts), len(weights), len(scratch)

    def kern(*refs):
        x_hbm, c_refs = refs[0], refs[1:1 + n_c]
        w_hbms, o_hbm = refs[1 + n_c:1 + n_c + n_w], refs[1 + n_c + n_w]
        rest = refs[2 + n_c + n_w:]
        s_refs, w_scratch = rest[:n_s], rest[n_s:n_s + 3 * n_w]
        xbuf, obuf, sem_in, sem_out = rest[n_s + 3 * n_w:]
        i = pl.program_id(0)
        slot = lax.rem(i, 2)

        def copies_in(step, sl):
            return [pltpu.make_async_copy(x_hbm.at[b, pl.ds(step * tt, tt), :],
                                          xbuf.at[sl, :, b, :], sem_in.at[sl]) for b in range(batch)]

        def copies_out(step, sl):
            return [pltpu.make_async_copy(obuf.at[sl, :, b, :],
                                          o_hbm.at[b, pl.ds(step * tt, tt), :], sem_out.at[sl])
                    for b in range(batch)]

        @pl.when(i == 0)
        def _():
            for c in copies_in(0, 0):
                c.start()
            _stage_weights(weights, w_hbms, w_scratch)

        @pl.when(i + 1 < n_steps)
        def _():
            for c in copies_in(i + 1, 1 - slot):
                c.start()

        for c in copies_in(i, slot):
            c.wait()

        @pl.when(i >= 2)
        def _():
            for c in copies_out(i - 2, slot):
                c.wait()

        body(xbuf.at[slot], *c_refs, *w_scratch[0::3], obuf.at[slot], *s_refs)

        for c in copies_out(i, slot):
            c.start()

        @pl.when(i == n_steps - 1)
        def _():
            for c in copies_out(i - 1, 1 - slot) + copies_out(i, slot):
                c.wait()

    any_spec = pl.BlockSpec(memory_space=pl.ANY)
    return pl.pallas_call(
        kern,
        grid=(n_steps,),
        in_specs=[any_spec] + [_const_spec(a.shape) for a in consts] + [any_spec] * n_w,
        out_specs=any_spec,
        out_shape=jax.ShapeDtypeStruct(x.shape, _F32),
        scratch_shapes=list(scratch) + _weight_scratch(weights) + [
            pltpu.VMEM((2, tt, batch, d), _F32),
            pltpu.VMEM((2, tt, batch, d), _F32),
            pltpu.SemaphoreType.DMA((2,)),
            pltpu.SemaphoreType.DMA((2,)),
        ],
        compiler_params=_params(1),
        name=name,
    )(x, *consts, *[w for w, _ in weights])


def _ab_kernel(x_ref, g_ref, cw_ref, cb_ref, wg_ref, bgx_ref, bga_ref, lam_ref,
               wp_ref, bp_ref, sc_ref, win_ref, wout_ref, o_ref,
               zrec_buf, pool_buf, a_buf, bx_buf, h_buf, h_carry, *, wa, wb, n_heads, cb_w):
    i = pl.program_id(0)
    tt = x_ref.shape[0]
    rows = tt * SUBLANES
    conv_k = cw_ref.shape[0]
    halo_c = (conv_k - 1) * SUBLANES
    halo_p = POOL_WINDOWS[-1] * SUBLANES
    hd = wa // n_heads
    hdb = wb // len(POOL_WINDOWS)

    @pl.when(i == 0)
    def _():
        zrec_buf[0:halo_c, :] = jnp.zeros((halo_c, wa), _F32)
        pool_buf[0:halo_p, :] = jnp.zeros((halo_p, wb), _F32)
        h_carry[...] = jnp.zeros_like(h_carry)

    x = _load_time_major(x_ref)
    xn = _rms(x, g_ref[...]).astype(_BF)
    sp = _softplus(-lam_ref[...])

    def in_proj(p):
        return (_dot(xn, win_ref[:, wa + p * cb_w:wa + (p + 1) * cb_w]),
                _dot(xn, win_ref[:, p * cb_w:(p + 1) * cb_w]))

    def pooled(z_pool):
        pool_buf[halo_p:halo_p + rows, :] = z_pool
        t_idx = i * tt + lax.broadcasted_iota(jnp.int32, (rows, 1), 0) // SUBLANES
        yb = []
        for g, w in enumerate(POOL_WINDOWS):
            gs = slice(g * hdb, (g + 1) * hdb)
            cur = pool_buf[:, gs]
            shift = 1
            while shift < w:
                cur = cur[shift * SUBLANES:, :] + cur[:cur.shape[0] - shift * SUBLANES, :]
                shift *= 2
            win_sum = cur[cur.shape[0] - rows:, :]
            cnt = jnp.minimum(t_idx + 1, w).astype(_F32)
            pm = (win_sum / cnt - z_pool[:, gs]).astype(_BF)
            yb.append((_dot(pm, wp_ref[g]) + bp_ref[:, gs]) * sc_ref[:, gs])
        pool_buf[0:halo_p, :] = pool_buf[rows:rows + halo_p, :]
        return jnp.concatenate(yb, axis=1).astype(_BF)

    out = x
    nxt = in_proj(0)
    n_blocks = wa // cb_w
    for p in range(n_blocks):
        cs = slice(p * cb_w, (p + 1) * cb_w)
        z_rec, z_gate = nxt

        zrec_buf[halo_c:halo_c + rows, cs] = z_rec
        xr = cb_ref[:, cs] + cw_ref[0:1, cs] * zrec_buf[0:rows, cs]
        for k in range(1, conv_k):
            xr = xr + cw_ref[k:k + 1, cs] * zrec_buf[k * SUBLANES:k * SUBLANES + rows, cs]
        zrec_buf[0:halo_c, cs] = zrec_buf[rows:rows + halo_c, cs]

        xr_b = xr.astype(_BF)
        heads = range(p * (cb_w // hd), (p + 1) * (cb_w // hd))
        gates = [_dot(xr_b[:, (h_idx * hd) % cb_w:(h_idx * hd) % cb_w + hd], wg_ref[h_idx])
                 for h_idx in heads]
        if p + 1 < n_blocks:
            nxt = in_proj(p + 1)
        for h_idx, gg in zip(heads, gates):
            hs = slice(h_idx * hd, (h_idx + 1) * hd)
            ls = slice((h_idx * hd) % cb_w, (h_idx * hd) % cb_w + hd)
            gate_x = _sigmoid(gg[:, :hd] + bgx_ref[:, hs])
            gate_a = _sigmoid(gg[:, hd:] + bga_ref[:, hs])
            a = jnp.exp((-C_RG) * gate_a * sp[:, hs])
            a_buf[:, hs] = a
            bx_buf[:, hs] = jnp.sqrt(1.0 - a * a) * (gate_x * xr[:, ls])

        h = h_carry[:, cs]
        for t in range(tt):
            ts = slice(t * SUBLANES, (t + 1) * SUBLANES)
            h = a_buf[ts, cs] * h + bx_buf[ts, cs]
            h_buf[ts, cs] = h
        h_carry[:, cs] = h

        y_a = (_gelu_tanh(z_gate) * h_buf[:, cs]).astype(_BF)
        out = out + _dot(y_a, wout_ref[cs, :])
        if p == 0:
            z_pool = _dot(xn, win_ref[:, 2 * wa:2 * wa + wb])
        elif p == 1:
            y_b = pooled(z_pool)
        elif p == 2:
            out = out + _dot(y_b, wout_ref[wa:wa + wb, :])

    _store_time_major(o_ref, out)


def _mixer_ab(x, norm, w_in, conv_w, conv_b, w_gx, b_gx, w_ga, b_ga, lam, w_pool, b_pool,
              pool_scale, w_out, *, tt):
    d = x.shape[2]
    n_heads, hd, _ = w_gx.shape
    wa = n_heads * hd
    wb = w_pool.shape[0] * w_pool.shape[1]
    rows = tt * SUBLANES
    halo_c = (conv_w.shape[0] - 1) * SUBLANES
    halo_p = POOL_WINDOWS[-1] * SUBLANES
    assert rows >= halo_p and w_in[0].shape[1:] == (d, 2 * wa + wb)
    wg = jnp.concatenate([w_gx, w_ga], axis=-1).astype(_BF)
    consts = (_row(norm), conv_w, _row(conv_b), wg, _row(b_gx), _row(b_ga),
              _row(lam), w_pool.astype(_BF), _row(b_pool), _row(pool_scale))
    scratch = [
        pltpu.VMEM((halo_c + rows, wa), _F32),
        pltpu.VMEM((halo_p + rows, wb), _F32),
        pltpu.VMEM((rows, wa), _F32),
        pltpu.VMEM((rows, wa), _F32),
        pltpu.VMEM((rows, wa), _F32),
        pltpu.VMEM((SUBLANES, wa), _F32),
    ]
    body = functools.partial(_ab_kernel, wa=wa, wb=wb, n_heads=n_heads, cb_w=2 * hd)
    return _time_tiled_call(body, x, consts, (w_in, w_out), scratch, tt=tt, name="mixer_ab")


def _conf_kernel(x_ref, g_ref, b1_ref, dw_ref, db_ref, lg_ref, lb_ref, b2_ref, w1_ref, w2_ref,
                 o_ref, glu_buf, conv_buf, *, chunk):
    i = pl.program_id(0)
    tt, d = x_ref.shape[0], x_ref.shape[2]
    rows = tt * SUBLANES
    conv_k = dw_ref.shape[0]
    halo = (conv_k - 1) * SUBLANES

    @pl.when(i == 0)
    def _():
        glu_buf[0:halo, 0:d] = jnp.zeros((halo, d), _F32)

    x = _load_time_major(x_ref)
    xn = _rms(x, g_ref[...]).astype(_BF)
    ha = _dot(xn, w1_ref[:, 0:d]) + b1_ref[:, 0:d]
    hb = _dot(xn, w1_ref[:, d:2 * d]) + b1_ref[:, d:2 * d]
    glu_buf[halo:halo + rows, 0:d] = ha * _sigmoid(hb)

    def conv_chunk(c, carry):
        r0 = pl.multiple_of(c * chunk, chunk)
        for j in range(d // LANES):
            ls = slice(j * LANES, (j + 1) * LANES)
            acc = db_ref[:, ls] + dw_ref[0:1, ls] * glu_buf[pl.ds(r0, chunk), ls]
            for k in range(1, conv_k):
                acc = acc + dw_ref[k:k + 1, ls] * glu_buf[pl.ds(r0 + k * SUBLANES, chunk), ls]
            conv_buf[pl.ds(r0, chunk), ls] = acc
        return carry

    lax.fori_loop(0, rows // chunk, conv_chunk, 0)
    glu_buf[0:halo, 0:d] = glu_buf[rows:rows + halo, 0:d]

    c = conv_buf[...]
    mu = jnp.mean(c, axis=-1, keepdims=True)
    cc = c - mu
    var = jnp.mean(cc * cc, axis=-1, keepdims=True)
    y = cc * lax.rsqrt(var + EPS) * lg_ref[...] + lb_ref[...]
    y = (y * _sigmoid(y)).astype(_BF)
    _store_time_major(o_ref, x + _dot(y, w2_ref[...]) + b2_ref[...])


def _conformer(x, norm, w1, b1, dw_w, dw_b, ln_g, ln_b, w2, b2, *, tt, chunk=64):
    d = x.shape[2]
    rows = tt * SUBLANES
    halo = (dw_w.shape[0] - 1) * SUBLANES
    assert rows >= halo and rows % chunk == 0
    consts = (_row(norm), _row(b1), dw_w, _row(dw_b), _row(ln_g), _row(ln_b), _row(b2))
    scratch = [pltpu.VMEM((halo + rows, d + LANES), _F32), pltpu.VMEM((rows, d), _F32)]
    body = functools.partial(_conf_kernel, chunk=chunk)
    return _time_tiled_call(body, x, consts, (w1, w2), scratch, tt=tt, name="conformer")


def _kv_kernel(m_ref, g_ref, wk_hbm, wv_hbm, k_ref, v_ref, *w_scratch, weights):
    @pl.when(pl.program_id(0) == 0)
    def _():
        _stage_weights(weights, (wk_hbm, wv_hbm), w_scratch)

    wk_ref, wv_ref = w_scratch[0::3]
    m = _rms(m_ref[0], g_ref[...]).astype(_BF)
    k_ref[0] = _dot(m, wk_ref[...]).astype(_BF)
    v_ref[0] = _dot(m, wv_ref[...]).astype(_BF)


def _mem_kv(mem, mem_norm, wk, wv):
    b, m, d = mem.shape
    blk = pl.BlockSpec((1, m, d), lambda i: (i, 0, 0))
    any_spec = pl.BlockSpec(memory_space=pl.ANY)
    return pl.pallas_call(
        functools.partial(_kv_kernel, weights=(wk, wv)),
        grid=(b,),
        in_specs=[blk, _const_spec((1, d)), any_spec, any_spec],
        out_specs=[blk, blk],
        out_shape=[jax.ShapeDtypeStruct((b, m, d), _BF)] * 2,
        scratch_shapes=_weight_scratch((wk, wv)),
        compiler_params=_params(1),
        name="mem_kv",
    )(mem, _row(mem_norm), wk[0], wv[0])


def _xa_kernel(x_ref, g_ref, k_ref, v_ref, wq_hbm, wo_hbm, o_ref, *w_scratch, n_heads, weights):
    @pl.when((pl.program_id(0) == 0) & (pl.program_id(1) == 0))
    def _():
        _stage_weights(weights, (wq_hbm, wo_hbm), w_scratch)

    wq_ref, wo_ref = w_scratch[0::3]
    d = x_ref.shape[2]
    hd = d // n_heads
    x = x_ref[0]
    xn = _rms(x, g_ref[...]).astype(_BF)
    q = (_dot(xn, wq_ref[...]) * (hd ** -0.5)).astype(_BF)
    outs = []
    for h in range(n_heads):
        hs = slice(h * hd, (h + 1) * hd)
        s = lax.dot_general(q[:, hs], k_ref[0, :, hs], (((1,), (1,)), ((), ())),
                            preferred_element_type=_F32)
        e = jnp.exp(s - jnp.max(s, axis=-1, keepdims=True))
        inv = 1.0 / jnp.sum(e, axis=-1, keepdims=True)
        outs.append(_dot(e.astype(_BF), v_ref[0, :, hs]) * inv)
    o = jnp.concatenate(outs, axis=1).astype(_BF)
    o_ref[0] = x + _dot(o, wo_ref[...])


def _cross_attn(x, k, v, norm, wq, wo, *, n_heads, ts):
    batch, seq, d = x.shape
    hd = d // n_heads
    assert seq % ts == 0 and math.log2(hd) % 2 == 0
    m = k.shape[1]
    tile = pl.BlockSpec((1, ts, d), lambda b, i: (b, i, 0))
    kv = pl.BlockSpec((1, m, d), lambda b, i: (b, 0, 0))
    any_spec = pl.BlockSpec(memory_space=pl.ANY)
    return pl.pallas_call(
        functools.partial(_xa_kernel, n_heads=n_heads, weights=(wq, wo)),
        grid=(batch, seq // ts),
        in_specs=[tile, _const_spec((1, d)), kv, kv, any_spec, any_spec],
        out_specs=tile,
        out_shape=jax.ShapeDtypeStruct(x.shape, _F32),
        scratch_shapes=_weight_scratch((wq, wo)),
        compiler_params=_params(2),
        name="cross_attn",
    )(x, _row(norm), k, v, wq[0], wo[0])


def _ffn_kernel(x_ref, g_ref, dw_ref, db_ref, fin_ref, wup_ref, wdn_ref, o_ref, g_halo,
                *, fc, final_norm):
    i = pl.program_id(0)
    tt = x_ref.shape[0]
    rows = tt * SUBLANES
    d_ff = wdn_ref.shape[0]
    conv_k = dw_ref.shape[0]
    halo = (conv_k - 1) * SUBLANES

    @pl.when(i == 0)
    def _():
        g_halo[...] = jnp.zeros_like(g_halo)

    x = _load_time_major(x_ref)
    xn = _rms(x, g_ref[...]).astype(_BF)

    def up(c):
        return (_dot(xn, wup_ref[:, c * fc:(c + 1) * fc]),
                _dot(xn, wup_ref[:, d_ff + c * fc:d_ff + (c + 1) * fc]))

    acc = x
    n_chunks = d_ff // fc
    nxt = up(0)
    for c in range(n_chunks):
        cs = slice(c * fc, (c + 1) * fc)
        g, u = nxt
        if c + 1 < n_chunks:
            nxt = up(c + 1)
        g_ext = jnp.concatenate([g_halo[:, cs], g], axis=0)
        g_halo[:, cs] = g[rows - halo:, :]
        conv = db_ref[:, cs] + dw_ref[0:1, cs] * g_ext[0:rows, :]
        for k in range(1, conv_k):
            conv = conv + dw_ref[k:k + 1, cs] * g_ext[k * SUBLANES:k * SUBLANES + rows, :]
        act = (_gelu_tanh(conv) * u).astype(_BF)
        acc = acc + _dot(act, wdn_ref[cs, :])
    if final_norm:
        acc = _rms(acc, fin_ref[...])
    _store_time_major(o_ref, acc)


def _conv_ffn(x, norm, w_up, dw_w, dw_b, w_down, fin, *, tt, fc=1024, final_norm=False):
    d_ff = w_down[0].shape[1]
    rows = tt * SUBLANES
    halo = (dw_w.shape[0] - 1) * SUBLANES
    assert d_ff % fc == 0 and rows >= halo
    consts = (_row(norm), dw_w, _row(dw_b), _row(fin))
    scratch = [pltpu.VMEM((halo, d_ff), _F32)]
    body = functools.partial(_ffn_kernel, fc=fc, final_norm=final_norm)
    return _time_tiled_call(body, x, consts, (w_up, w_down), scratch, tt=tt, name="conv_ffn")


def kernel(x, mem, ab_norm, ab_w_in, a_conv_w, a_conv_b, a_gate_x_w, a_gate_x_b, a_gate_a_w, a_gate_a_b, a_lambda, b_group_w, b_group_b, b_scale, ab_w_out, c_norm, c_w_pw1, c_b_pw1, c_dw_w, c_dw_b, c_ln_g, c_ln_b, c_w_pw2, c_b_pw2, xa_norm, xa_mem_norm, xa_wq, xa_wk, xa_wv, xa_wo, f_norm, f_w_up, f_dw_w, f_dw_b, f_w_down, final_norm):
    depth = xa_norm.shape[0]
    tt = 64
    for layer in range(depth):
        if layer % 2 == 0:
            i = layer // 2
            x = _mixer_ab(x, ab_norm[i], (ab_w_in, i), a_conv_w[i], a_conv_b[i], a_gate_x_w[i],
                          a_gate_x_b[i], a_gate_a_w[i], a_gate_a_b[i], a_lambda[i],
                          b_group_w[i], b_group_b[i], b_scale[i], (ab_w_out, i), tt=tt)
        else:
            j = layer // 2
            x = _conformer(x, c_norm[j], (c_w_pw1, j), c_b_pw1[j], c_dw_w[j], c_dw_b[j],
                           c_ln_g[j], c_ln_b[j], (c_w_pw2, j), c_b_pw2[j], tt=tt)
        k, v = _mem_kv(mem, xa_mem_norm[layer], (xa_wk, layer), (xa_wv, layer))
        x = _cross_attn(x, k, v, xa_norm[layer], (xa_wq, layer), (xa_wo, layer),
                        n_heads=XA_HEADS, ts=1024)
        x = _conv_ffn(x, f_norm[layer], (f_w_up, layer), f_dw_w[layer], f_dw_b[layer],
                      (f_w_down, layer), final_norm, tt=tt, final_norm=(layer == depth - 1))
    return x
```

```python
import functools
import math

import jax
import jax.numpy as jnp
from jax import lax
from jax.experimental import pallas as pl
from jax.experimental.pallas import tpu as pltpu

EPS = 1e-6
C_RG = 8.0
POOL_WINDOWS = (2, 4, 8, 16)
XA_HEADS = 4
SUBLANES = 8
LANES = 128
VMEM_LIMIT_BYTES = 56 * 1024 * 1024

_BF = jnp.bfloat16
_F32 = jnp.float32


def _dot(a, b):
    return jnp.dot(a, b, preferred_element_type=_F32)


def _rms(x, g):
    return x * lax.rsqrt(jnp.mean(x * x, axis=-1, keepdims=True) + EPS) * g


def _sigmoid(x):
    return 1.0 / (1.0 + jnp.exp(-x))


def _gelu_tanh(x):
    c = math.sqrt(2.0 / math.pi)
    return 0.5 * x * (1.0 + jnp.tanh(c * (x + 0.044715 * (x * x * x))))


def _softplus(x):
    return jnp.maximum(x, 0.0) + jnp.log1p(jnp.exp(-jnp.abs(x)))


def _load_time_major(x_ref):
    tt, b, d = x_ref.shape
    return x_ref[...].reshape(tt * b, d)


def _store_time_major(o_ref, v):
    o_ref[...] = v.reshape(o_ref.shape)


def _const_spec(shape):
    nd = len(shape)
    return pl.BlockSpec(shape, lambda *_: (0,) * nd, pipeline_mode=pl.Buffered(1))


def _params(n_grid):
    return pltpu.CompilerParams(
        dimension_semantics=("arbitrary",) * n_grid,
        vmem_limit_bytes=VMEM_LIMIT_BYTES,
    )


def _row(v):
    return v.reshape(1, -1)


STAGE_BYTES = 1 << 20
STAGE_SLOTS = 4


def _stage_rows(n_rows, n_cols):
    rc = 1 << int(math.log2(max(SUBLANES, STAGE_BYTES // (4 * n_cols))))
    while n_rows % rc:
        rc //= 2
    assert rc >= SUBLANES
    return rc


def _stage_to_bf16(w_hbm, dst, stage, sem):
    n_slots, rc, _ = stage.shape
    copies = [pltpu.make_async_copy(w_hbm.at[pl.ds(k * rc, rc), :], stage.at[k % n_slots],
                                    sem.at[k % n_slots]) for k in range(dst.shape[0] // rc)]
    ahead = n_slots - 1
    for c in copies[:ahead]:
        c.start()
    for k, c in enumerate(copies):
        if k + ahead < len(copies):
            copies[k + ahead].start()
        c.wait()
        dst[k * rc:(k + 1) * rc, :] = stage[k % n_slots].astype(_BF)


def _weight_scratch(weights):
    out = []
    for w, _ in weights:
        _, r, c = w.shape
        out += [pltpu.VMEM((r, c), _BF), pltpu.VMEM((STAGE_SLOTS, _stage_rows(r, c), c), _F32),
                pltpu.SemaphoreType.DMA((STAGE_SLOTS,))]
    return out


def _stage_weights(weights, w_hbms, w_scratch):
    for k, (_, layer) in enumerate(weights):
        dst, stage, sem = w_scratch[3 * k:3 * k + 3]
        _stage_to_bf16(w_hbms[k].at[layer], dst, stage, sem)


def _time_tiled_call(body, x, consts, weights, scratch, *, tt, name):
    batch, seq, d = x.shape
    n_steps = seq // tt
    assert batch == SUBLANES and seq % tt == 0 and n_steps >= 2
    n_c, n_w, n_s = len(consts), len(weights), len(scratch)

    def kern(*refs):
        x_hbm, c_refs = refs[0], refs[1:1 + n_c]
        w_hbms, o_hbm = refs[1 + n_c:1 + n_c + n_w], refs[1 + n_c + n_w]
        rest = refs[2 + n_c + n_w:]
        s_refs, w_scratch = rest[:n_s], rest[n_s:n_s + 3 * n_w]
        xbuf, obuf, sem_in, sem_out = rest[n_s + 3 * n_w:]
        i = pl.program_id(0)
        slot = lax.rem(i, 2)

        def copies_in(step, sl):
            return [pltpu.make_async_copy(x_hbm.at[b, pl.ds(step * tt, tt), :],
                                          xbuf.at[sl, :, b, :], sem_in.at[sl]) for b in range(batch)]

        def copies_out(step, sl):
            return [pltpu.make_async_copy(obuf.at[sl, :, b, :],
                                          o_hbm.at[b, pl.ds(step * tt, tt), :], sem_out.at[sl])
                    for b in range(batch)]

        @pl.when(i == 0)
        def _():
            for c in copies_in(0, 0):
                c.start()
            _stage_weights(weights, w_hbms, w_scratch)

        @pl.when(i + 1 < n_steps)
        def _():
            for c in copies_in(i + 1, 1 - slot):
                c.start()

        for c in copies_in(i, slot):
            c.wait()

        @pl.when(i >= 2)
        def _():
            for c in copies_out(i - 2, slot):
                c.wait()

        body(xbuf.at[slot], *c_refs, *w_scratch[0::3], obuf.at[slot], *s_refs)

        for c in copies_out(i, slot):
            c.start()

        @pl.when(i == n_steps - 1)
        def _():
            for c in copies_out(i - 1, 1 - slot) + copies_out(i, slot):
                c.wait()

    any_spec = pl.BlockSpec(memory_space=pl.ANY)
    return pl.pallas_call(
        kern,
        grid=(n_steps,),
        in_specs=[any_spec] + [_const_spec(a.shape) for a in consts] + [any_spec] * n_w,
        out_specs=any_spec,
        out_shape=jax.ShapeDtypeStruct(x.shape, _F32),
        scratch_shapes=list(scratch) + _weight_scratch(weights) + [
            pltpu.VMEM((2, tt, batch, d), _F32),
            pltpu.VMEM((2, tt, batch, d), _F32),
            pltpu.SemaphoreType.DMA((2,)),
            pltpu.SemaphoreType.DMA((2,)),
        ],
        compiler_params=_params(1),
        name=name,
    )(x, *consts, *[w for w, _ in weights])


def _ab_kernel(x_ref, g_ref, cw_ref, cb_ref, wg_ref, bgx_ref, bga_ref, lam_ref,
               wp_ref, bp_ref, sc_ref, win_ref, wout_ref, o_ref,
               zrec_buf, pool_buf, a_buf, bx_buf, h_buf, h_carry, *, wa, wb, n_heads, cb_w):
    i = pl.program_id(0)
    tt = x_ref.shape[0]
    rows = tt * SUBLANES
    conv_k = cw_ref.shape[0]
    halo_c = (conv_k - 1) * SUBLANES
    halo_p = POOL_WINDOWS[-1] * SUBLANES
    hd = wa // n_heads
    hdb = wb // len(POOL_WINDOWS)

    @pl.when(i == 0)
    def _():
        zrec_buf[0:halo_c, :] = jnp.zeros((halo_c, wa), _F32)
        pool_buf[0:halo_p, :] = jnp.zeros((halo_p, wb), _F32)
        h_carry[...] = jnp.zeros_like(h_carry)

    x = _load_time_major(x_ref)
    xn = _rms(x, g_ref[...]).astype(_BF)
    sp = _softplus(-lam_ref[...])

    def in_proj(p):
        return (_dot(xn, win_ref[:, wa + p * cb_w:wa + (p + 1) * cb_w]),
                _dot(xn, win_ref[:, p * cb_w:(p + 1) * cb_w]))

    def pooled(z_pool):
        pool_buf[halo_p:halo_p + rows, :] = z_pool
        t_idx = i * tt + lax.broadcasted_iota(jnp.int32, (rows, 1), 0) // SUBLANES
        yb = []
        for g, w in enumerate(POOL_WINDOWS):
            gs = slice(g * hdb, (g + 1) * hdb)
            cur = pool_buf[:, gs]
            shift = 1
            while shift < w:
                cur = cur[shift * SUBLANES:, :] + cur[:cur.shape[0] - shift * SUBLANES, :]
                shift *= 2
            win_sum = cur[cur.shape[0] - rows:, :]
            cnt = jnp.minimum(t_idx + 1, w).astype(_F32)
            pm = (win_sum / cnt - z_pool[:, gs]).astype(_BF)
            yb.append((_dot(pm, wp_ref[g]) + bp_ref[:, gs]) * sc_ref[:, gs])
        pool_buf[0:halo_p, :] = pool_buf[rows:rows + halo_p, :]
        return jnp.concatenate(yb, axis=1).astype(_BF)

    out = x
    nxt = in_proj(0)
    n_blocks = wa // cb_w
    for p in range(n_blocks):
        cs = slice(p * cb_w, (p + 1) * cb_w)
        z_rec, z_gate = nxt

        zrec_buf[halo_c:halo_c + rows, cs] = z_rec
        xr = cb_ref[:, cs] + cw_ref[0:1, cs] * zrec_buf[0:rows, cs]
        for k in range(1, conv_k):
            xr = xr + cw_ref[k:k + 1, cs] * zrec_buf[k * SUBLANES:k * SUBLANES + rows, cs]
        zrec_buf[0:halo_c, cs] = zrec_buf[rows:rows + halo_c, cs]

        xr_b = xr.astype(_BF)
        heads = range(p * (cb_w // hd), (p + 1) * (cb_w // hd))
        gates = [_dot(xr_b[:, (h_idx * hd) % cb_w:(h_idx * hd) % cb_w + hd], wg_ref[h_idx])
                 for h_idx in heads]
        if p + 1 < n_blocks:
            nxt = in_proj(p + 1)
        for h_idx, gg in zip(heads, gates):
            hs = slice(h_idx * hd, (h_idx + 1) * hd)
            ls = slice((h_idx * hd) % cb_w, (h_idx * hd) % cb_w + hd)
            gate_x = _sigmoid(gg[:, :hd] + bgx_ref[:, hs])
            gate_a = _sigmoid(gg[:, hd:] + bga_ref[:, hs])
            a = jnp.exp((-C_RG) * gate_a * sp[:, hs])
            a_buf[:, hs] = a
            bx_buf[:, hs] = jnp.sqrt(1.0 - a * a) * (gate_x * xr[:, ls])

        h = h_carry[:, cs]
        for t in range(tt):
            ts = slice(t * SUBLANES, (t + 1) * SUBLANES)
            h = a_buf[ts, cs] * h + bx_buf[ts, cs]
            h_buf[ts, cs] = h
        h_carry[:, cs] = h

        y_a = (_gelu_tanh(z_gate) * h_buf[:, cs]).astype(_BF)
        out = out + _dot(y_a, wout_ref[cs, :])
        if p == 0:
            z_pool = _dot(xn, win_ref[:, 2 * wa:2 * wa + wb])
        elif p == 1:
            y_b = pooled(z_pool)
        elif p == 2:
            out = out + _dot(y_b, wout_ref[wa:wa + wb, :])

    _store_time_major(o_ref, out)


def _mixer_ab(x, norm, w_in, conv_w, conv_b, w_gx, b_gx, w_ga, b_ga, lam, w_pool, b_pool,
              pool_scale, w_out, *, tt):
    d = x.shape[2]
    n_heads, hd, _ = w_gx.shape
    wa = n_heads * hd
    wb = w_pool.shape[0] * w_pool.shape[1]
    rows = tt * SUBLANES
    halo_c = (conv_w.shape[0] - 1) * SUBLANES
    halo_p = POOL_WINDOWS[-1] * SUBLANES
    assert rows >= halo_p and w_in[0].shape[1:] == (d, 2 * wa + wb)
    wg = jnp.concatenate([w_gx, w_ga], axis=-1).astype(_BF)
    consts = (_row(norm), conv_w, _row(conv_b), wg, _row(b_gx), _row(b_ga),
              _row(lam), w_pool.astype(_BF), _row(b_pool), _row(pool_scale))
    scratch = [
        pltpu.VMEM((halo_c + rows, wa), _F32),
        pltpu.VMEM((halo_p + rows, wb), _F32),
        pltpu.VMEM((rows, wa), _F32),
        pltpu.VMEM((rows, wa), _F32),
        pltpu.VMEM((rows, wa), _F32),
        pltpu.VMEM((SUBLANES, wa), _F32),
    ]
    body = functools.partial(_ab_kernel, wa=wa, wb=wb, n_heads=n_heads, cb_w=2 * hd)
    return _time_tiled_call(body, x, consts, (w_in, w_out), scratch, tt=tt, name="mixer_ab")


CONV_CHUNKS_PER_TRIP = 2


def _conf_kernel(x_ref, g_ref, b1_ref, dw_ref, db_ref, lg_ref, lb_ref, b2_ref, w1_ref, w2_ref,
                 o_ref, glu_buf, conv_buf, *, chunk):
    i = pl.program_id(0)
    tt, d = x_ref.shape[0], x_ref.shape[2]
    rows = tt * SUBLANES
    conv_k = dw_ref.shape[0]
    halo = (conv_k - 1) * SUBLANES

    @pl.when(i == 0)
    def _():
        glu_buf[0:halo, 0:d] = jnp.zeros((halo, d), _F32)

    x = _load_time_major(x_ref)
    xn = _rms(x, g_ref[...]).astype(_BF)
    ha = _dot(xn, w1_ref[:, 0:d]) + b1_ref[:, 0:d]
    hb = _dot(xn, w1_ref[:, d:2 * d]) + b1_ref[:, d:2 * d]
    glu_buf[halo:halo + rows, 0:d] = ha * _sigmoid(hb)

    def conv_trip(c, carry):
        for s in range(CONV_CHUNKS_PER_TRIP):
            r0 = pl.multiple_of((c * CONV_CHUNKS_PER_TRIP + s) * chunk, chunk)
            for j in range(d // LANES):
                ls = slice(j * LANES, (j + 1) * LANES)
                acc = db_ref[:, ls] + dw_ref[0:1, ls] * glu_buf[pl.ds(r0, chunk), ls]
                for k in range(1, conv_k):
                    acc = acc + dw_ref[k:k + 1, ls] * glu_buf[pl.ds(r0 + k * SUBLANES, chunk), ls]
                conv_buf[pl.ds(r0, chunk), ls] = acc
        return carry

    lax.fori_loop(0, rows // (chunk * CONV_CHUNKS_PER_TRIP), conv_trip, 0)
    glu_buf[0:halo, 0:d] = glu_buf[rows:rows + halo, 0:d]

    c = conv_buf[...]
    mu = jnp.mean(c, axis=-1, keepdims=True)
    cc = c - mu
    var = jnp.mean(cc * cc, axis=-1, keepdims=True)
    y = cc * lax.rsqrt(var + EPS) * lg_ref[...] + lb_ref[...]
    y = (y * _sigmoid(y)).astype(_BF)
    _store_time_major(o_ref, x + _dot(y, w2_ref[...]) + b2_ref[...])


def _conformer(x, norm, w1, b1, dw_w, dw_b, ln_g, ln_b, w2, b2, *, tt, chunk=64):
    d = x.shape[2]
    rows = tt * SUBLANES
    halo = (dw_w.shape[0] - 1) * SUBLANES
    assert rows >= halo and rows % (chunk * CONV_CHUNKS_PER_TRIP) == 0
    consts = (_row(norm), _row(b1), dw_w, _row(dw_b), _row(ln_g), _row(ln_b), _row(b2))
    scratch = [pltpu.VMEM((halo + rows, d + LANES), _F32), pltpu.VMEM((rows, d), _F32)]
    body = functools.partial(_conf_kernel, chunk=chunk)
    return _time_tiled_call(body, x, consts, (w1, w2), scratch, tt=tt, name="conformer")


def _kv_kernel(m_ref, g_ref, wk_hbm, wv_hbm, k_ref, v_ref, *w_scratch, weights):
    @pl.when(pl.program_id(0) == 0)
    def _():
        _stage_weights(weights, (wk_hbm, wv_hbm), w_scratch)

    wk_ref, wv_ref = w_scratch[0::3]
    m = _rms(m_ref[0], g_ref[...]).astype(_BF)
    k_ref[0] = _dot(m, wk_ref[...]).astype(_BF)
    v_ref[0] = _dot(m, wv_ref[...]).astype(_BF)


def _mem_kv(mem, mem_norm, wk, wv):
    b, m, d = mem.shape
    blk = pl.BlockSpec((1, m, d), lambda i: (i, 0, 0))
    any_spec = pl.BlockSpec(memory_space=pl.ANY)
    return pl.pallas_call(
        functools.partial(_kv_kernel, weights=(wk, wv)),
        grid=(b,),
        in_specs=[blk, _const_spec((1, d)), any_spec, any_spec],
        out_specs=[blk, blk],
        out_shape=[jax.ShapeDtypeStruct((b, m, d), _BF)] * 2,
        scratch_shapes=_weight_scratch((wk, wv)),
        compiler_params=_params(1),
        name="mem_kv",
    )(mem, _row(mem_norm), wk[0], wv[0])


def _xa_kernel(x_ref, g_ref, k_ref, v_ref, wq_hbm, wo_hbm, o_ref, *w_scratch, n_heads, weights):
    @pl.when((pl.program_id(0) == 0) & (pl.program_id(1) == 0))
    def _():
        _stage_weights(weights, (wq_hbm, wo_hbm), w_scratch)

    wq_ref, wo_ref = w_scratch[0::3]
    d = x_ref.shape[2]
    hd = d // n_heads
    x = x_ref[0]
    xn = _rms(x, g_ref[...]).astype(_BF)
    q = (_dot(xn, wq_ref[...]) * (hd ** -0.5)).astype(_BF)
    outs = []
    for h in range(n_heads):
        hs = slice(h * hd, (h + 1) * hd)
        s = lax.dot_general(q[:, hs], k_ref[0, :, hs], (((1,), (1,)), ((), ())),
                            preferred_element_type=_F32)
        e = jnp.exp(s - jnp.max(s, axis=-1, keepdims=True))
        inv = 1.0 / jnp.sum(e, axis=-1, keepdims=True)
        outs.append(_dot(e.astype(_BF), v_ref[0, :, hs]) * inv)
    o = jnp.concatenate(outs, axis=1).astype(_BF)
    o_ref[0] = x + _dot(o, wo_ref[...])


def _cross_attn(x, k, v, norm, wq, wo, *, n_heads, ts):
    batch, seq, d = x.shape
    hd = d // n_heads
    assert seq % ts == 0 and math.log2(hd) % 2 == 0
    m = k.shape[1]
    tile = pl.BlockSpec((1, ts, d), lambda b, i: (b, i, 0))
    kv = pl.BlockSpec((1, m, d), lambda b, i: (b, 0, 0))
    any_spec = pl.BlockSpec(memory_space=pl.ANY)
    return pl.pallas_call(
        functools.partial(_xa_kernel, n_heads=n_heads, weights=(wq, wo)),
        grid=(batch, seq // ts),
        in_specs=[tile, _const_spec((1, d)), kv, kv, any_spec, any_spec],
        out_specs=tile,
        out_shape=jax.ShapeDtypeStruct(x.shape, _F32),
        scratch_shapes=_weight_scratch((wq, wo)),
        compiler_params=_params(2),
        name="cross_attn",
    )(x, _row(norm), k, v, wq[0], wo[0])


def _ffn_kernel(x_ref, g_ref, dw_ref, db_ref, fin_ref, wup_ref, wdn_ref, o_ref, g_halo,
                *, fc, final_norm):
    i = pl.program_id(0)
    tt = x_ref.shape[0]
    rows = tt * SUBLANES
    d_ff = wdn_ref.shape[0]
    conv_k = dw_ref.shape[0]
    halo = (conv_k - 1) * SUBLANES

    @pl.when(i == 0)
    def _():
        g_halo[...] = jnp.zeros_like(g_halo)

    x = _load_time_major(x_ref)
    xn = _rms(x, g_ref[...]).astype(_BF)

    def up(c):
        return (_dot(xn, wup_ref[:, c * fc:(c + 1) * fc]),
                _dot(xn, wup_ref[:, d_ff + c * fc:d_ff + (c + 1) * fc]))

    acc = x
    n_chunks = d_ff // fc
    nxt = up(0)
    for c in range(n_chunks):
        cs = slice(c * fc, (c + 1) * fc)
        g, u = nxt
        if c + 1 < n_chunks:
            nxt = up(c + 1)
        g_ext = jnp.concatenate([g_halo[:, cs], g], axis=0)
        g_halo[:, cs] = g[rows - halo:, :]
        conv = db_ref[:, cs] + dw_ref[0:1, cs] * g_ext[0:rows, :]
        for k in range(1, conv_k):
            conv = conv + dw_ref[k:k + 1, cs] * g_ext[k * SUBLANES:k * SUBLANES + rows, :]
        act = (_gelu_tanh(conv) * u).astype(_BF)
        acc = acc + _dot(act, wdn_ref[cs, :])
    if final_norm:
        acc = _rms(acc, fin_ref[...])
    _store_time_major(o_ref, acc)


def _conv_ffn(x, norm, w_up, dw_w, dw_b, w_down, fin, *, tt, fc=1024, final_norm=False):
    d_ff = w_down[0].shape[1]
    rows = tt * SUBLANES
    halo = (dw_w.shape[0] - 1) * SUBLANES
    assert d_ff % fc == 0 and rows >= halo
    consts = (_row(norm), dw_w, _row(dw_b), _row(fin))
    scratch = [pltpu.VMEM((halo, d_ff), _F32)]
    body = functools.partial(_ffn_kernel, fc=fc, final_norm=final_norm)
    return _time_tiled_call(body, x, consts, (w_up, w_down), scratch, tt=tt, name="conv_ffn")


def kernel(x, mem, ab_norm, ab_w_in, a_conv_w, a_conv_b, a_gate_x_w, a_gate_x_b, a_gate_a_w, a_gate_a_b, a_lambda, b_group_w, b_group_b, b_scale, ab_w_out, c_norm, c_w_pw1, c_b_pw1, c_dw_w, c_dw_b, c_ln_g, c_ln_b, c_w_pw2, c_b_pw2, xa_norm, xa_mem_norm, xa_wq, xa_wk, xa_wv, xa_wo, f_norm, f_w_up, f_dw_w, f_dw_b, f_w_down, final_norm):
    depth = xa_norm.shape[0]
    tt = 64
    for layer in range(depth):
        if layer % 2 == 0:
            i = layer // 2
            x = _mixer_ab(x, ab_norm[i], (ab_w_in, i), a_conv_w[i], a_conv_b[i], a_gate_x_w[i],
                          a_gate_x_b[i], a_gate_a_w[i], a_gate_a_b[i], a_lambda[i],
                          b_group_w[i], b_group_b[i], b_scale[i], (ab_w_out, i), tt=tt)
        else:
            j = layer // 2
            x = _conformer(x, c_norm[j], (c_w_pw1, j), c_b_pw1[j], c_dw_w[j], c_dw_b[j],
                           c_ln_g[j], c_ln_b[j], (c_w_pw2, j), c_b_pw2[j], tt=tt)
        k, v = _mem_kv(mem, xa_mem_norm[layer], (xa_wk, layer), (xa_wv, layer))
        x = _cross_attn(x, k, v, xa_norm[layer], (xa_wq, layer), (xa_wo, layer),
                        n_heads=XA_HEADS, ts=1024)
        x = _conv_ffn(x, f_norm[layer], (f_w_up, layer), f_dw_w[layer], f_dw_b[layer],
                      (f_w_down, layer), final_norm, tt=tt, final_norm=(layer == depth - 1))
    return x
```

```python
import functools
import math

import jax
import jax.numpy as jnp
from jax import lax
from jax.experimental import pallas as pl
from jax.experimental.pallas import tpu as pltpu

EPS = 1e-6
C_RG = 8.0
POOL_WINDOWS = (2, 4, 8, 16)
XA_HEADS = 4
SUBLANES = 8
LANES = 128
VMEM_LIMIT_BYTES = 56 * 1024 * 1024

_BF = jnp.bfloat16
_F32 = jnp.float32


def _dot(a, b):
    return jnp.dot(a, b, preferred_element_type=_F32)


def _rms(x, g):
    return x * lax.rsqrt(jnp.mean(x * x, axis=-1, keepdims=True) + EPS) * g


def _sigmoid(x):
    return 1.0 / (1.0 + jnp.exp(-x))


def _gelu_tanh(x):
    c = math.sqrt(2.0 / math.pi)
    return 0.5 * x * (1.0 + jnp.tanh(c * (x + 0.044715 * (x * x * x))))


def _softplus(x):
    return jnp.maximum(x, 0.0) + jnp.log1p(jnp.exp(-jnp.abs(x)))


def _load_time_major(x_ref):
    tt, b, d = x_ref.shape
    return x_ref[...].reshape(tt * b, d)


def _store_time_major(o_ref, v):
    o_ref[...] = v.reshape(o_ref.shape)


def _const_spec(shape):
    nd = len(shape)
    return pl.BlockSpec(shape, lambda *_: (0,) * nd, pipeline_mode=pl.Buffered(1))


def _params(n_grid):
    return pltpu.CompilerParams(
        dimension_semantics=("arbitrary",) * n_grid,
        vmem_limit_bytes=VMEM_LIMIT_BYTES,
    )


def _row(v):
    return v.reshape(1, -1)


STAGE_BYTES = 1 << 20
STAGE_SLOTS = 4


def _stage_rows(n_rows, n_cols):
    rc = 1 << int(math.log2(max(SUBLANES, STAGE_BYTES // (4 * n_cols))))
    while n_rows % rc:
        rc //= 2
    assert rc >= SUBLANES
    return rc


def _stage_to_bf16(w_hbm, dst, stage, sem):
    n_slots, rc, _ = stage.shape
    copies = [pltpu.make_async_copy(w_hbm.at[pl.ds(k * rc, rc), :], stage.at[k % n_slots],
                                    sem.at[k % n_slots]) for k in range(dst.shape[0] // rc)]
    ahead = n_slots - 1
    for c in copies[:ahead]:
        c.start()
    for k, c in enumerate(copies):
        if k + ahead < len(copies):
            copies[k + ahead].start()
        c.wait()
        dst[k * rc:(k + 1) * rc, :] = stage[k % n_slots].astype(_BF)


def _weight_scratch(weights):
    out = []
    for w, _ in weights:
        _, r, c = w.shape
        out += [pltpu.VMEM((r, c), _BF), pltpu.VMEM((STAGE_SLOTS, _stage_rows(r, c), c), _F32),
                pltpu.SemaphoreType.DMA((STAGE_SLOTS,))]
    return out


def _stage_weights(weights, w_hbms, w_scratch):
    for k, (_, layer) in enumerate(weights):
        dst, stage, sem = w_scratch[3 * k:3 * k + 3]
        _stage_to_bf16(w_hbms[k].at[layer], dst, stage, sem)


def _time_tiled_call(body, x, consts, weights, scratch, *, tt, name):
    batch, seq, d = x.shape
    n_steps = seq // tt
    assert batch == SUBLANES and seq % tt == 0 and n_steps >= 2
    n_c, n_w, n_s = len(consts), len(weights), len(scratch)

    def kern(*refs):
        x_hbm, c_refs = refs[0], refs[1:1 + n_c]
        w_hbms, o_hbm = refs[1 + n_c:1 + n_c + n_w], refs[1 + n_c + n_w]
        rest = refs[2 + n_c + n_w:]
        s_refs, w_scratch = rest[:n_s], rest[n_s:n_s + 3 * n_w]
        xbuf, obuf, sem_in, sem_out = rest[n_s + 3 * n_w:]
        i = pl.program_id(0)
        slot = lax.rem(i, 2)

        def copies_in(step, sl):
            return [pltpu.make_async_copy(x_hbm.at[b, pl.ds(step * tt, tt), :],
                                          xbuf.at[sl, :, b, :], sem_in.at[sl]) for b in range(batch)]

        def copies_out(step, sl):
            return [pltpu.make_async_copy(obuf.at[sl, :, b, :],
                                          o_hbm.at[b, pl.ds(step * tt, tt), :], sem_out.at[sl])
                    for b in range(batch)]

        @pl.when(i == 0)
        def _():
            for c in copies_in(0, 0):
                c.start()
            _stage_weights(weights, w_hbms, w_scratch)

        @pl.when(i + 1 < n_steps)
        def _():
            for c in copies_in(i + 1, 1 - slot):
                c.start()

        for c in copies_in(i, slot):
            c.wait()

        @pl.when(i >= 2)
        def _():
            for c in copies_out(i - 2, slot):
                c.wait()

        body(xbuf.at[slot], *c_refs, *w_scratch[0::3], obuf.at[slot], *s_refs)

        for c in copies_out(i, slot):
            c.start()

        @pl.when(i == n_steps - 1)
        def _():
            for c in copies_out(i - 1, 1 - slot) + copies_out(i, slot):
                c.wait()

    any_spec = pl.BlockSpec(memory_space=pl.ANY)
    return pl.pallas_call(
        kern,
        grid=(n_steps,),
        in_specs=[any_spec] + [_const_spec(a.shape) for a in consts] + [any_spec] * n_w,
        out_specs=any_spec,
        out_shape=jax.ShapeDtypeStruct(x.shape, _F32),
        scratch_shapes=list(scratch) + _weight_scratch(weights) + [
            pltpu.VMEM((2, tt, batch, d), _F32),
            pltpu.VMEM((2, tt, batch, d), _F32),
            pltpu.SemaphoreType.DMA((2,)),
            pltpu.SemaphoreType.DMA((2,)),
        ],
        compiler_params=_params(1),
        name=name,
    )(x, *consts, *[w for w, _ in weights])


def _ab_kernel(x_ref, g_ref, cw_ref, cb_ref, wg_ref, bgx_ref, bga_ref, lam_ref,
               wp_ref, bp_ref, sc_ref, win_ref, wout_ref, o_ref,
               zrec_buf, pool_buf, a_buf, bx_buf, h_buf, h_carry, *, wa, wb, n_heads, cb_w):
    i = pl.program_id(0)
    tt = x_ref.shape[0]
    rows = tt * SUBLANES
    conv_k = cw_ref.shape[0]
    halo_c = (conv_k - 1) * SUBLANES
    halo_p = POOL_WINDOWS[-1] * SUBLANES
    hd = wa // n_heads
    hdb = wb // len(POOL_WINDOWS)

    @pl.when(i == 0)
    def _():
        zrec_buf[0:halo_c, :] = jnp.zeros((halo_c, wa), _F32)
        pool_buf[0:halo_p, :] = jnp.zeros((halo_p, wb), _F32)
        h_carry[...] = jnp.zeros_like(h_carry)

    x = _load_time_major(x_ref)
    xn = _rms(x, g_ref[...]).astype(_BF)
    sp = _softplus(-lam_ref[...])

    def in_proj(p):
        return (_dot(xn, win_ref[:, wa + p * cb_w:wa + (p + 1) * cb_w]),
                _dot(xn, win_ref[:, p * cb_w:(p + 1) * cb_w]))

    def pooled(z_pool):
        pool_buf[halo_p:halo_p + rows, :] = z_pool
        t_idx = i * tt + lax.broadcasted_iota(jnp.int32, (rows, 1), 0) // SUBLANES
        yb = []
        for g, w in enumerate(POOL_WINDOWS):
            gs = slice(g * hdb, (g + 1) * hdb)
            cur = pool_buf[:, gs]
            shift = 1
            while shift < w:
                cur = cur[shift * SUBLANES:, :] + cur[:cur.shape[0] - shift * SUBLANES, :]
                shift *= 2
            win_sum = cur[cur.shape[0] - rows:, :]
            cnt = jnp.minimum(t_idx + 1, w).astype(_F32)
            pm = (win_sum / cnt - z_pool[:, gs]).astype(_BF)
            yb.append((_dot(pm, wp_ref[g]) + bp_ref[:, gs]) * sc_ref[:, gs])
        pool_buf[0:halo_p, :] = pool_buf[rows:rows + halo_p, :]
        return jnp.concatenate(yb, axis=1).astype(_BF)

    out = x
    nxt = in_proj(0)
    n_blocks = wa // cb_w
    for p in range(n_blocks):
        cs = slice(p * cb_w, (p + 1) * cb_w)
        z_rec, z_gate = nxt

        zrec_buf[halo_c:halo_c + rows, cs] = z_rec
        xr = cb_ref[:, cs] + cw_ref[0:1, cs] * zrec_buf[0:rows, cs]
        for k in range(1, conv_k):
            xr = xr + cw_ref[k:k + 1, cs] * zrec_buf[k * SUBLANES:k * SUBLANES + rows, cs]
        zrec_buf[0:halo_c, cs] = zrec_buf[rows:rows + halo_c, cs]

        xr_b = xr.astype(_BF)
        heads = range(p * (cb_w // hd), (p + 1) * (cb_w // hd))
        gates = [_dot(xr_b[:, (h_idx * hd) % cb_w:(h_idx * hd) % cb_w + hd], wg_ref[h_idx])
                 for h_idx in heads]
        if p + 1 < n_blocks:
            nxt = in_proj(p + 1)
        for h_idx, gg in zip(heads, gates):
            hs = slice(h_idx * hd, (h_idx + 1) * hd)
            ls = slice((h_idx * hd) % cb_w, (h_idx * hd) % cb_w + hd)
            gate_x = _sigmoid(gg[:, :hd] + bgx_ref[:, hs])
            gate_a = _sigmoid(gg[:, hd:] + bga_ref[:, hs])
            a = jnp.exp((-C_RG) * gate_a * sp[:, hs])
            a_buf[:, hs] = a
            bx_buf[:, hs] = jnp.sqrt(1.0 - a * a) * (gate_x * xr[:, ls])

        h = h_carry[:, cs]
        for t in range(tt):
            ts = slice(t * SUBLANES, (t + 1) * SUBLANES)
            h = a_buf[ts, cs] * h + bx_buf[ts, cs]
            h_buf[ts, cs] = h
        h_carry[:, cs] = h

        y_a = (_gelu_tanh(z_gate) * h_buf[:, cs]).astype(_BF)
        out = out + _dot(y_a, wout_ref[cs, :])
        if p == 0:
            z_pool = _dot(xn, win_ref[:, 2 * wa:2 * wa + wb])
        elif p == 1:
            y_b = pooled(z_pool)
        elif p == 2:
            out = out + _dot(y_b, wout_ref[wa:wa + wb, :])

    _store_time_major(o_ref, out)


def _mixer_ab(x, norm, w_in, conv_w, conv_b, w_gx, b_gx, w_ga, b_ga, lam, w_pool, b_pool,
              pool_scale, w_out, *, tt):
    d = x.shape[2]
    n_heads, hd, _ = w_gx.shape
    wa = n_heads * hd
    wb = w_pool.shape[0] * w_pool.shape[1]
    rows = tt * SUBLANES
    halo_c = (conv_w.shape[0] - 1) * SUBLANES
    halo_p = POOL_WINDOWS[-1] * SUBLANES
    assert rows >= halo_p and w_in[0].shape[1:] == (d, 2 * wa + wb)
    wg = jnp.concatenate([w_gx, w_ga], axis=-1).astype(_BF)
    consts = (_row(norm), conv_w, _row(conv_b), wg, _row(b_gx), _row(b_ga),
              _row(lam), w_pool.astype(_BF), _row(b_pool), _row(pool_scale))
    scratch = [
        pltpu.VMEM((halo_c + rows, wa), _F32),
        pltpu.VMEM((halo_p + rows, wb), _F32),
        pltpu.VMEM((rows, wa), _F32),
        pltpu.VMEM((rows, wa), _F32),
        pltpu.VMEM((rows, wa), _F32),
        pltpu.VMEM((SUBLANES, wa), _F32),
    ]
    body = functools.partial(_ab_kernel, wa=wa, wb=wb, n_heads=n_heads, cb_w=2 * hd)
    return _time_tiled_call(body, x, consts, (w_in, w_out), scratch, tt=tt, name="mixer_ab")


CONV_CHUNKS_PER_TRIP = 2


def _conf_kernel(x_ref, g_ref, b1_ref, dw_ref, db_ref, lg_ref, lb_ref, b2_ref, w1_ref, w2_ref,
                 o_ref, glu_buf, conv_buf, *, chunk):
    i = pl.program_id(0)
    tt, d = x_ref.shape[0], x_ref.shape[2]
    rows = tt * SUBLANES
    conv_k = dw_ref.shape[0]
    halo = (conv_k - 1) * SUBLANES

    @pl.when(i == 0)
    def _():
        glu_buf[0:halo, 0:d] = jnp.zeros((halo, d), _F32)

    x = _load_time_major(x_ref)
    hr = rows // 2
    for r0 in (0, hr):
        xn = _rms(x[r0:r0 + hr], g_ref[...]).astype(_BF)
        ha = _dot(xn, w1_ref[:, 0:d]) + b1_ref[:, 0:d]
        hb = _dot(xn, w1_ref[:, d:2 * d]) + b1_ref[:, d:2 * d]
        glu_buf[halo + r0:halo + r0 + hr, 0:d] = ha * _sigmoid(hb)

    def conv_trip(c, carry):
        for s in range(CONV_CHUNKS_PER_TRIP):
            r0 = pl.multiple_of((c * CONV_CHUNKS_PER_TRIP + s) * chunk, chunk)
            for j in range(d // LANES):
                ls = slice(j * LANES, (j + 1) * LANES)
                acc = db_ref[:, ls] + dw_ref[0:1, ls] * glu_buf[pl.ds(r0, chunk), ls]
                for k in range(1, conv_k):
                    acc = acc + dw_ref[k:k + 1, ls] * glu_buf[pl.ds(r0 + k * SUBLANES, chunk), ls]
                conv_buf[pl.ds(r0, chunk), ls] = acc
        return carry

    lax.fori_loop(0, rows // (chunk * CONV_CHUNKS_PER_TRIP), conv_trip, 0)
    glu_buf[0:halo, 0:d] = glu_buf[rows:rows + halo, 0:d]

    c = conv_buf[...]
    mu = jnp.mean(c, axis=-1, keepdims=True)
    cc = c - mu
    var = jnp.mean(cc * cc, axis=-1, keepdims=True)
    y = cc * lax.rsqrt(var + EPS) * lg_ref[...] + lb_ref[...]
    y = (y * _sigmoid(y)).astype(_BF)
    _store_time_major(o_ref, x + _dot(y, w2_ref[...]) + b2_ref[...])


def _conformer(x, norm, w1, b1, dw_w, dw_b, ln_g, ln_b, w2, b2, *, tt, chunk=64):
    d = x.shape[2]
    rows = tt * SUBLANES
    halo = (dw_w.shape[0] - 1) * SUBLANES
    assert rows >= halo and rows % (chunk * CONV_CHUNKS_PER_TRIP) == 0
    consts = (_row(norm), _row(b1), dw_w, _row(dw_b), _row(ln_g), _row(ln_b), _row(b2))
    scratch = [pltpu.VMEM((halo + rows, d + LANES), _F32), pltpu.VMEM((rows, d), _F32)]
    body = functools.partial(_conf_kernel, chunk=chunk)
    return _time_tiled_call(body, x, consts, (w1, w2), scratch, tt=tt, name="conformer")


def _kv_kernel(m_ref, g_ref, wk_hbm, wv_hbm, k_ref, v_ref, *w_scratch, weights):
    @pl.when(pl.program_id(0) == 0)
    def _():
        _stage_weights(weights, (wk_hbm, wv_hbm), w_scratch)

    wk_ref, wv_ref = w_scratch[0::3]
    m = _rms(m_ref[0], g_ref[...]).astype(_BF)
    k_ref[0] = _dot(m, wk_ref[...]).astype(_BF)
    v_ref[0] = _dot(m, wv_ref[...]).astype(_BF)


def _mem_kv(mem, mem_norm, wk, wv):
    b, m, d = mem.shape
    blk = pl.BlockSpec((1, m, d), lambda i: (i, 0, 0))
    any_spec = pl.BlockSpec(memory_space=pl.ANY)
    return pl.pallas_call(
        functools.partial(_kv_kernel, weights=(wk, wv)),
        grid=(b,),
        in_specs=[blk, _const_spec((1, d)), any_spec, any_spec],
        out_specs=[blk, blk],
        out_shape=[jax.ShapeDtypeStruct((b, m, d), _BF)] * 2,
        scratch_shapes=_weight_scratch((wk, wv)),
        compiler_params=_params(1),
        name="mem_kv",
    )(mem, _row(mem_norm), wk[0], wv[0])


def _xa_kernel(x_ref, g_ref, k_ref, v_ref, wq_hbm, wo_hbm, o_ref, *w_scratch, n_heads, weights):
    @pl.when((pl.program_id(0) == 0) & (pl.program_id(1) == 0))
    def _():
        _stage_weights(weights, (wq_hbm, wo_hbm), w_scratch)

    wq_ref, wo_ref = w_scratch[0::3]
    d = x_ref.shape[2]
    hd = d // n_heads
    x = x_ref[0]
    hr = x.shape[0] // 2
    q = jnp.concatenate(
        [(_dot(_rms(x[r0:r0 + hr], g_ref[...]).astype(_BF), wq_ref[...]) * (hd ** -0.5)).astype(_BF)
         for r0 in (0, hr)], axis=0)
    outs = []
    for h in range(n_heads):
        hs = slice(h * hd, (h + 1) * hd)
        s = lax.dot_general(q[:, hs], k_ref[0, :, hs], (((1,), (1,)), ((), ())),
                            preferred_element_type=_F32)
        e = jnp.exp(s - jnp.max(s, axis=-1, keepdims=True))
        inv = 1.0 / jnp.sum(e, axis=-1, keepdims=True)
        outs.append(_dot(e.astype(_BF), v_ref[0, :, hs]) * inv)
    o = jnp.concatenate(outs, axis=1).astype(_BF)
    o_ref[0] = x + _dot(o, wo_ref[...])


def _cross_attn(x, k, v, norm, wq, wo, *, n_heads, ts):
    batch, seq, d = x.shape
    hd = d // n_heads
    assert seq % ts == 0 and math.log2(hd) % 2 == 0
    m = k.shape[1]
    tile = pl.BlockSpec((1, ts, d), lambda b, i: (b, i, 0))
    kv = pl.BlockSpec((1, m, d), lambda b, i: (b, 0, 0))
    any_spec = pl.BlockSpec(memory_space=pl.ANY)
    return pl.pallas_call(
        functools.partial(_xa_kernel, n_heads=n_heads, weights=(wq, wo)),
        grid=(batch, seq // ts),
        in_specs=[tile, _const_spec((1, d)), kv, kv, any_spec, any_spec],
        out_specs=tile,
        out_shape=jax.ShapeDtypeStruct(x.shape, _F32),
        scratch_shapes=_weight_scratch((wq, wo)),
        compiler_params=_params(2),
        name="cross_attn",
    )(x, _row(norm), k, v, wq[0], wo[0])


def _ffn_kernel(x_ref, g_ref, dw_ref, db_ref, fin_ref, wup_ref, wdn_ref, o_ref, g_halo,
                *, fc, final_norm):
    i = pl.program_id(0)
    tt = x_ref.shape[0]
    rows = tt * SUBLANES
    d_ff = wdn_ref.shape[0]
    conv_k = dw_ref.shape[0]
    halo = (conv_k - 1) * SUBLANES

    @pl.when(i == 0)
    def _():
        g_halo[...] = jnp.zeros_like(g_halo)

    x = _load_time_major(x_ref)
    hr = rows // 2
    xn_a = _rms(x[:hr], g_ref[...]).astype(_BF)
    xn_b = _rms(x[hr:], g_ref[...]).astype(_BF)

    def up(c):
        wg = wup_ref[:, c * fc:(c + 1) * fc]
        wu = wup_ref[:, d_ff + c * fc:d_ff + (c + 1) * fc]
        return (jnp.concatenate([_dot(xn_a, wg), _dot(xn_b, wg)], axis=0),
                jnp.concatenate([_dot(xn_a, wu), _dot(xn_b, wu)], axis=0))

    acc = x
    n_chunks = d_ff // fc
    nxt = up(0)
    for c in range(n_chunks):
        cs = slice(c * fc, (c + 1) * fc)
        g, u = nxt
        if c + 1 < n_chunks:
            nxt = up(c + 1)
        g_ext = jnp.concatenate([g_halo[:, cs], g], axis=0)
        g_halo[:, cs] = g[rows - halo:, :]
        conv = db_ref[:, cs] + dw_ref[0:1, cs] * g_ext[0:rows, :]
        for k in range(1, conv_k):
            conv = conv + dw_ref[k:k + 1, cs] * g_ext[k * SUBLANES:k * SUBLANES + rows, :]
        act = (_gelu_tanh(conv) * u).astype(_BF)
        acc = acc + _dot(act, wdn_ref[cs, :])
    if final_norm:
        acc = _rms(acc, fin_ref[...])
    _store_time_major(o_ref, acc)


def _conv_ffn(x, norm, w_up, dw_w, dw_b, w_down, fin, *, tt, fc=1024, final_norm=False):
    d_ff = w_down[0].shape[1]
    rows = tt * SUBLANES
    halo = (dw_w.shape[0] - 1) * SUBLANES
    assert d_ff % fc == 0 and rows >= halo
    consts = (_row(norm), dw_w, _row(dw_b), _row(fin))
    scratch = [pltpu.VMEM((halo, d_ff), _F32)]
    body = functools.partial(_ffn_kernel, fc=fc, final_norm=final_norm)
    return _time_tiled_call(body, x, consts, (w_up, w_down), scratch, tt=tt, name="conv_ffn")


def kernel(x, mem, ab_norm, ab_w_in, a_conv_w, a_conv_b, a_gate_x_w, a_gate_x_b, a_gate_a_w, a_gate_a_b, a_lambda, b_group_w, b_group_b, b_scale, ab_w_out, c_norm, c_w_pw1, c_b_pw1, c_dw_w, c_dw_b, c_ln_g, c_ln_b, c_w_pw2, c_b_pw2, xa_norm, xa_mem_norm, xa_wq, xa_wk, xa_wv, xa_wo, f_norm, f_w_up, f_dw_w, f_dw_b, f_w_down, final_norm):
    depth = xa_norm.shape[0]
    tt = 64
    for layer in range(depth):
        if layer % 2 == 0:
            i = layer // 2
            x = _mixer_ab(x, ab_norm[i], (ab_w_in, i), a_conv_w[i], a_conv_b[i], a_gate_x_w[i],
                          a_gate_x_b[i], a_gate_a_w[i], a_gate_a_b[i], a_lambda[i],
                          b_group_w[i], b_group_b[i], b_scale[i], (ab_w_out, i), tt=tt)
        else:
            j = layer // 2
            x = _conformer(x, c_norm[j], (c_w_pw1, j), c_b_pw1[j], c_dw_w[j], c_dw_b[j],
                           c_ln_g[j], c_ln_b[j], (c_w_pw2, j), c_b_pw2[j], tt=tt)
        k, v = _mem_kv(mem, xa_mem_norm[layer], (xa_wk, layer), (xa_wv, layer))
        x = _cross_attn(x, k, v, xa_norm[layer], (xa_wq, layer), (xa_wo, layer),
                        n_heads=XA_HEADS, ts=1024)
        x = _conv_ffn(x, f_norm[layer], (f_w_up, layer), f_dw_w[layer], f_dw_b[layer],
                      (f_w_down, layer), final_norm, tt=tt, final_norm=(layer == depth - 1))
    return x
```

```python
import functools
import math

import jax
import jax.numpy as jnp
from jax import lax
from jax.experimental import pallas as pl
from jax.experimental.pallas import tpu as pltpu

EPS = 1e-6
C_RG = 8.0
POOL_WINDOWS = (2, 4, 8, 16)
XA_HEADS = 4
SUBLANES = 8
LANES = 128
VMEM_LIMIT_BYTES = 56 * 1024 * 1024

_BF = jnp.bfloat16
_F32 = jnp.float32


def _dot(a, b):
    return jnp.dot(a, b, preferred_element_type=_F32)


def _rms(x, g):
    return x * lax.rsqrt(jnp.mean(x * x, axis=-1, keepdims=True) + EPS) * g


def _sigmoid(x):
    return 1.0 / (1.0 + jnp.exp(-x))


def _gelu_tanh(x):
    c = math.sqrt(2.0 / math.pi)
    return 0.5 * x * (1.0 + jnp.tanh(c * (x + 0.044715 * (x * x * x))))


def _softplus(x):
    return jnp.maximum(x, 0.0) + jnp.log1p(jnp.exp(-jnp.abs(x)))


def _load_time_major(x_ref):
    tt, b, d = x_ref.shape
    return x_ref[...].reshape(tt * b, d)


def _store_time_major(o_ref, v):
    o_ref[...] = v.reshape(o_ref.shape)


def _const_spec(shape):
    nd = len(shape)
    return pl.BlockSpec(shape, lambda *_: (0,) * nd, pipeline_mode=pl.Buffered(1))


def _params(n_grid):
    return pltpu.CompilerParams(
        dimension_semantics=("arbitrary",) * n_grid,
        vmem_limit_bytes=VMEM_LIMIT_BYTES,
    )


def _row(v):
    return v.reshape(1, -1)


STAGE_BYTES = 1 << 20
STAGE_SLOTS = 4


def _stage_rows(n_rows, n_cols):
    rc = 1 << int(math.log2(max(SUBLANES, STAGE_BYTES // (4 * n_cols))))
    while n_rows % rc:
        rc //= 2
    assert rc >= SUBLANES
    return rc


def _stage_to_bf16(w_hbm, dst, stage, sem):
    n_slots, rc, _ = stage.shape
    copies = [pltpu.make_async_copy(w_hbm.at[pl.ds(k * rc, rc), :], stage.at[k % n_slots],
                                    sem.at[k % n_slots]) for k in range(dst.shape[0] // rc)]
    ahead = n_slots - 1
    for c in copies[:ahead]:
        c.start()
    for k, c in enumerate(copies):
        if k + ahead < len(copies):
            copies[k + ahead].start()
        c.wait()
        dst[k * rc:(k + 1) * rc, :] = stage[k % n_slots].astype(_BF)


def _weight_scratch(weights):
    out = []
    for w, _ in weights:
        _, r, c = w.shape
        out += [pltpu.VMEM((r, c), _BF), pltpu.VMEM((STAGE_SLOTS, _stage_rows(r, c), c), _F32),
                pltpu.SemaphoreType.DMA((STAGE_SLOTS,))]
    return out


def _stage_weights(weights, w_hbms, w_scratch):
    for k, (_, layer) in enumerate(weights):
        dst, stage, sem = w_scratch[3 * k:3 * k + 3]
        _stage_to_bf16(w_hbms[k].at[layer], dst, stage, sem)


def _time_tiled_call(body, x, consts, weights, scratch, *, tt, name):
    batch, seq, d = x.shape
    n_steps = seq // tt
    assert batch == SUBLANES and seq % tt == 0 and n_steps >= 2
    n_c, n_w, n_s = len(consts), len(weights), len(scratch)

    def kern(*refs):
        x_hbm, c_refs = refs[0], refs[1:1 + n_c]
        w_hbms, o_hbm = refs[1 + n_c:1 + n_c + n_w], refs[1 + n_c + n_w]
        rest = refs[2 + n_c + n_w:]
        s_refs, w_scratch = rest[:n_s], rest[n_s:n_s + 3 * n_w]
        xbuf, obuf, sem_in, sem_out = rest[n_s + 3 * n_w:]
        i = pl.program_id(0)
        slot = lax.rem(i, 2)

        def copies_in(step, sl):
            return [pltpu.make_async_copy(x_hbm.at[b, pl.ds(step * tt, tt), :],
                                          xbuf.at[sl, :, b, :], sem_in.at[sl]) for b in range(batch)]

        def copies_out(step, sl):
            return [pltpu.make_async_copy(obuf.at[sl, :, b, :],
                                          o_hbm.at[b, pl.ds(step * tt, tt), :], sem_out.at[sl])
                    for b in range(batch)]

        @pl.when(i == 0)
        def _():
            for c in copies_in(0, 0):
                c.start()
            _stage_weights(weights, w_hbms, w_scratch)

        @pl.when(i + 1 < n_steps)
        def _():
            for c in copies_in(i + 1, 1 - slot):
                c.start()

        for c in copies_in(i, slot):
            c.wait()

        @pl.when(i >= 2)
        def _():
            for c in copies_out(i - 2, slot):
                c.wait()

        body(xbuf.at[slot], *c_refs, *w_scratch[0::3], obuf.at[slot], *s_refs)

        for c in copies_out(i, slot):
            c.start()

        @pl.when(i == n_steps - 1)
        def _():
            for c in copies_out(i - 1, 1 - slot) + copies_out(i, slot):
                c.wait()

    any_spec = pl.BlockSpec(memory_space=pl.ANY)
    return pl.pallas_call(
        kern,
        grid=(n_steps,),
        in_specs=[any_spec] + [_const_spec(a.shape) for a in consts] + [any_spec] * n_w,
        out_specs=any_spec,
        out_shape=jax.ShapeDtypeStruct(x.shape, _F32),
        scratch_shapes=list(scratch) + _weight_scratch(weights) + [
            pltpu.VMEM((2, tt, batch, d), _F32),
            pltpu.VMEM((2, tt, batch, d), _F32),
            pltpu.SemaphoreType.DMA((2,)),
            pltpu.SemaphoreType.DMA((2,)),
        ],
        compiler_params=_params(1),
        name=name,
    )(x, *consts, *[w for w, _ in weights])


def _ab_kernel(x_ref, g_ref, cw_ref, cb_ref, wg_ref, bgx_ref, bga_ref, lam_ref,
               wp_ref, bp_ref, sc_ref, win_ref, wout_ref, o_ref,
               zrec_buf, pool_buf, a_buf, bx_buf, h_buf, h_carry, *, wa, wb, n_heads, cb_w):
    i = pl.program_id(0)
    tt = x_ref.shape[0]
    rows = tt * SUBLANES
    conv_k = cw_ref.shape[0]
    halo_c = (conv_k - 1) * SUBLANES
    halo_p = POOL_WINDOWS[-1] * SUBLANES
    hd = wa // n_heads
    hdb = wb // len(POOL_WINDOWS)

    @pl.when(i == 0)
    def _():
        zrec_buf[0:halo_c, :] = jnp.zeros((halo_c, wa), _F32)
        pool_buf[0:halo_p, :] = jnp.zeros((halo_p, wb), _F32)
        h_carry[...] = jnp.zeros_like(h_carry)

    x = _load_time_major(x_ref)
    xn = _rms(x, g_ref[...]).astype(_BF)
    sp = _softplus(-lam_ref[...])

    def in_proj(p):
        return (_dot(xn, win_ref[:, wa + p * cb_w:wa + (p + 1) * cb_w]),
                _dot(xn, win_ref[:, p * cb_w:(p + 1) * cb_w]))

    def pooled(z_pool):
        pool_buf[halo_p:halo_p + rows, :] = z_pool
        t_idx = i * tt + lax.broadcasted_iota(jnp.int32, (rows, 1), 0) // SUBLANES
        yb = []
        for g, w in enumerate(POOL_WINDOWS):
            gs = slice(g * hdb, (g + 1) * hdb)
            cur = pool_buf[:, gs]
            shift = 1
            while shift < w:
                cur = cur[shift * SUBLANES:, :] + cur[:cur.shape[0] - shift * SUBLANES, :]
                shift *= 2
            win_sum = cur[cur.shape[0] - rows:, :]
            cnt = jnp.minimum(t_idx + 1, w).astype(_F32)
            pm = (win_sum / cnt - z_pool[:, gs]).astype(_BF)
            yb.append((_dot(pm, wp_ref[g]) + bp_ref[:, gs]) * sc_ref[:, gs])
        pool_buf[0:halo_p, :] = pool_buf[rows:rows + halo_p, :]
        return jnp.concatenate(yb, axis=1).astype(_BF)

    out = x
    nxt = in_proj(0)
    n_blocks = wa // cb_w
    for p in range(n_blocks):
        cs = slice(p * cb_w, (p + 1) * cb_w)
        z_rec, z_gate = nxt

        zrec_buf[halo_c:halo_c + rows, cs] = z_rec
        xr = cb_ref[:, cs] + cw_ref[0:1, cs] * zrec_buf[0:rows, cs]
        for k in range(1, conv_k):
            xr = xr + cw_ref[k:k + 1, cs] * zrec_buf[k * SUBLANES:k * SUBLANES + rows, cs]
        zrec_buf[0:halo_c, cs] = zrec_buf[rows:rows + halo_c, cs]

        xr_b = xr.astype(_BF)
        heads = range(p * (cb_w // hd), (p + 1) * (cb_w // hd))
        gates = [_dot(xr_b[:, (h_idx * hd) % cb_w:(h_idx * hd) % cb_w + hd], wg_ref[h_idx])
                 for h_idx in heads]
        if p + 1 < n_blocks:
            nxt = in_proj(p + 1)
        for h_idx, gg in zip(heads, gates):
            hs = slice(h_idx * hd, (h_idx + 1) * hd)
            ls = slice((h_idx * hd) % cb_w, (h_idx * hd) % cb_w + hd)
            gate_x = _sigmoid(gg[:, :hd] + bgx_ref[:, hs])
            gate_a = _sigmoid(gg[:, hd:] + bga_ref[:, hs])
            a = jnp.exp((-C_RG) * gate_a * sp[:, hs])
            a_buf[:, hs] = a
            bx_buf[:, hs] = jnp.sqrt(1.0 - a * a) * (gate_x * xr[:, ls])

        h = h_carry[:, cs]
        for t in range(tt):
            ts = slice(t * SUBLANES, (t + 1) * SUBLANES)
            h = a_buf[ts, cs] * h + bx_buf[ts, cs]
            h_buf[ts, cs] = h
        h_carry[:, cs] = h

        y_a = (_gelu_tanh(z_gate) * h_buf[:, cs]).astype(_BF)
        out = out + _dot(y_a, wout_ref[cs, :])
        if p == 0:
            z_pool = _dot(xn, win_ref[:, 2 * wa:2 * wa + wb])
        elif p == 1:
            y_b = pooled(z_pool)
        elif p == 2:
            out = out + _dot(y_b, wout_ref[wa:wa + wb, :])

    _store_time_major(o_ref, out)


def _mixer_ab(x, norm, w_in, conv_w, conv_b, w_gx, b_gx, w_ga, b_ga, lam, w_pool, b_pool,
              pool_scale, w_out, *, tt):
    d = x.shape[2]
    n_heads, hd, _ = w_gx.shape
    wa = n_heads * hd
    wb = w_pool.shape[0] * w_pool.shape[1]
    rows = tt * SUBLANES
    halo_c = (conv_w.shape[0] - 1) * SUBLANES
    halo_p = POOL_WINDOWS[-1] * SUBLANES
    assert rows >= halo_p and w_in[0].shape[1:] == (d, 2 * wa + wb)
    wg = jnp.concatenate([w_gx, w_ga], axis=-1).astype(_BF)
    consts = (_row(norm), conv_w, _row(conv_b), wg, _row(b_gx), _row(b_ga),
              _row(lam), w_pool.astype(_BF), _row(b_pool), _row(pool_scale))
    scratch = [
        pltpu.VMEM((halo_c + rows, wa), _F32),
        pltpu.VMEM((halo_p + rows, wb), _F32),
        pltpu.VMEM((rows, wa), _F32),
        pltpu.VMEM((rows, wa), _F32),
        pltpu.VMEM((rows, wa), _F32),
        pltpu.VMEM((SUBLANES, wa), _F32),
    ]
    body = functools.partial(_ab_kernel, wa=wa, wb=wb, n_heads=n_heads, cb_w=2 * hd)
    return _time_tiled_call(body, x, consts, (w_in, w_out), scratch, tt=tt, name="mixer_ab")


CONV_CHUNKS_PER_TRIP = 2


def _conf_kernel(x_ref, g_ref, b1_ref, dw_ref, db_ref, lg_ref, lb_ref, b2_ref, w1_ref, w2_ref,
                 o_ref, glu_buf, conv_buf, *, chunk):
    i = pl.program_id(0)
    tt, d = x_ref.shape[0], x_ref.shape[2]
    rows = tt * SUBLANES
    conv_k = dw_ref.shape[0]
    halo = (conv_k - 1) * SUBLANES

    @pl.when(i == 0)
    def _():
        glu_buf[0:halo, 0:d] = jnp.zeros((halo, d), _F32)

    x = _load_time_major(x_ref)
    xn = _rms(x, g_ref[...]).astype(_BF)
    ha = _dot(xn, w1_ref[:, 0:d]) + b1_ref[:, 0:d]
    hb = _dot(xn, w1_ref[:, d:2 * d]) + b1_ref[:, d:2 * d]
    glu_buf[halo:halo + rows, 0:d] = ha * _sigmoid(hb)

    def conv_trip(c, carry):
        for s in range(CONV_CHUNKS_PER_TRIP):
            r0 = pl.multiple_of((c * CONV_CHUNKS_PER_TRIP + s) * chunk, chunk)
            for j in range(d // LANES):
                ls = slice(j * LANES, (j + 1) * LANES)
                acc = db_ref[:, ls] + dw_ref[0:1, ls] * glu_buf[pl.ds(r0, chunk), ls]
                for k in range(1, conv_k):
                    acc = acc + dw_ref[k:k + 1, ls] * glu_buf[pl.ds(r0 + k * SUBLANES, chunk), ls]
                conv_buf[pl.ds(r0, chunk), ls] = acc
        return carry

    lax.fori_loop(0, rows // (chunk * CONV_CHUNKS_PER_TRIP), conv_trip, 0)
    glu_buf[0:halo, 0:d] = glu_buf[rows:rows + halo, 0:d]

    c = conv_buf[...]
    mu = jnp.mean(c, axis=-1, keepdims=True)
    cc = c - mu
    var = jnp.mean(cc * cc, axis=-1, keepdims=True)
    y = cc * lax.rsqrt(var + EPS) * lg_ref[...] + lb_ref[...]
    y = (y * _sigmoid(y)).astype(_BF)
    _store_time_major(o_ref, x + _dot(y, w2_ref[...]) + b2_ref[...])


def _conformer(x, norm, w1, b1, dw_w, dw_b, ln_g, ln_b, w2, b2, *, tt, chunk=64):
    d = x.shape[2]
    rows = tt * SUBLANES
    halo = (dw_w.shape[0] - 1) * SUBLANES
    assert rows >= halo and rows % (chunk * CONV_CHUNKS_PER_TRIP) == 0
    consts = (_row(norm), _row(b1), dw_w, _row(dw_b), _row(ln_g), _row(ln_b), _row(b2))
    scratch = [pltpu.VMEM((halo + rows, d + LANES), _F32), pltpu.VMEM((rows, d), _F32)]
    body = functools.partial(_conf_kernel, chunk=chunk)
    return _time_tiled_call(body, x, consts, (w1, w2), scratch, tt=tt, name="conformer")


def _kv_kernel(m_ref, g_ref, wk_hbm, wv_hbm, k_ref, v_ref, *w_scratch, weights):
    @pl.when(pl.program_id(0) == 0)
    def _():
        _stage_weights(weights, (wk_hbm, wv_hbm), w_scratch)

    wk_ref, wv_ref = w_scratch[0::3]
    m = _rms(m_ref[0], g_ref[...]).astype(_BF)
    k_ref[0] = _dot(m, wk_ref[...]).astype(_BF)
    v_ref[0] = _dot(m, wv_ref[...]).astype(_BF)


def _mem_kv(mem, mem_norm, wk, wv):
    b, m, d = mem.shape
    blk = pl.BlockSpec((1, m, d), lambda i: (i, 0, 0))
    any_spec = pl.BlockSpec(memory_space=pl.ANY)
    return pl.pallas_call(
        functools.partial(_kv_kernel, weights=(wk, wv)),
        grid=(b,),
        in_specs=[blk, _const_spec((1, d)), any_spec, any_spec],
        out_specs=[blk, blk],
        out_shape=[jax.ShapeDtypeStruct((b, m, d), _BF)] * 2,
        scratch_shapes=_weight_scratch((wk, wv)),
        compiler_params=_params(1),
        name="mem_kv",
    )(mem, _row(mem_norm), wk[0], wv[0])


def _xa_kernel(x_ref, g_ref, k_ref, v_ref, wq_hbm, wo_hbm, o_ref, *w_scratch, n_heads, weights):
    @pl.when((pl.program_id(0) == 0) & (pl.program_id(1) == 0))
    def _():
        _stage_weights(weights, (wq_hbm, wo_hbm), w_scratch)

    wq_ref, wo_ref = w_scratch[0::3]
    d = x_ref.shape[2]
    hd = d // n_heads
    x = x_ref[0]
    hr = x.shape[0] // 2
    q = jnp.concatenate(
        [(_dot(_rms(x[r0:r0 + hr], g_ref[...]).astype(_BF), wq_ref[...]) * (hd ** -0.5)).astype(_BF)
         for r0 in (0, hr)], axis=0)
    outs = []
    for h in range(n_heads):
        hs = slice(h * hd, (h + 1) * hd)
        s = lax.dot_general(q[:, hs], k_ref[0, :, hs], (((1,), (1,)), ((), ())),
                            preferred_element_type=_F32)
        e = jnp.exp(s - jnp.max(s, axis=-1, keepdims=True))
        inv = 1.0 / jnp.sum(e, axis=-1, keepdims=True)
        outs.append(_dot(e.astype(_BF), v_ref[0, :, hs]) * inv)
    o = jnp.concatenate(outs, axis=1).astype(_BF)
    o_ref[0] = x + _dot(o, wo_ref[...])


def _cross_attn(x, k, v, norm, wq, wo, *, n_heads, ts):
    batch, seq, d = x.shape
    hd = d // n_heads
    assert seq % ts == 0 and math.log2(hd) % 2 == 0
    m = k.shape[1]
    tile = pl.BlockSpec((1, ts, d), lambda b, i: (b, i, 0))
    kv = pl.BlockSpec((1, m, d), lambda b, i: (b, 0, 0))
    any_spec = pl.BlockSpec(memory_space=pl.ANY)
    return pl.pallas_call(
        functools.partial(_xa_kernel, n_heads=n_heads, weights=(wq, wo)),
        grid=(batch, seq // ts),
        in_specs=[tile, _const_spec((1, d)), kv, kv, any_spec, any_spec],
        out_specs=tile,
        out_shape=jax.ShapeDtypeStruct(x.shape, _F32),
        scratch_shapes=_weight_scratch((wq, wo)),
        compiler_params=_params(2),
        name="cross_attn",
    )(x, _row(norm), k, v, wq[0], wo[0])


def _ffn_kernel(x_ref, g_ref, dw_ref, db_ref, fin_ref, wup_ref, wdn_ref, o_ref, g_halo,
                *, fc, final_norm):
    i = pl.program_id(0)
    tt = x_ref.shape[0]
    rows = tt * SUBLANES
    d_ff = wdn_ref.shape[0]
    conv_k = dw_ref.shape[0]
    halo = (conv_k - 1) * SUBLANES

    @pl.when(i == 0)
    def _():
        g_halo[...] = jnp.zeros_like(g_halo)

    x = _load_time_major(x_ref)
    hr = rows // 2
    xn_a = _rms(x[:hr], g_ref[...]).astype(_BF)
    xn_b = _rms(x[hr:], g_ref[...]).astype(_BF)

    def up(c):
        wg = wup_ref[:, c * fc:(c + 1) * fc]
        wu = wup_ref[:, d_ff + c * fc:d_ff + (c + 1) * fc]
        return (jnp.concatenate([_dot(xn_a, wg), _dot(xn_b, wg)], axis=0),
                jnp.concatenate([_dot(xn_a, wu), _dot(xn_b, wu)], axis=0))

    acc = x
    n_chunks = d_ff // fc
    nxt = up(0)
    for c in range(n_chunks):
        cs = slice(c * fc, (c + 1) * fc)
        g, u = nxt
        if c + 1 < n_chunks:
            nxt = up(c + 1)
        g_ext = jnp.concatenate([g_halo[:, cs], g], axis=0)
        g_halo[:, cs] = g[rows - halo:, :]
        conv = db_ref[:, cs] + dw_ref[0:1, cs] * g_ext[0:rows, :]
        for k in range(1, conv_k):
            conv = conv + dw_ref[k:k + 1, cs] * g_ext[k * SUBLANES:k * SUBLANES + rows, :]
        act = (_gelu_tanh(conv) * u).astype(_BF)
        acc = acc + _dot(act, wdn_ref[cs, :])
    if final_norm:
        acc = _rms(acc, fin_ref[...])
    _store_time_major(o_ref, acc)


def _conv_ffn(x, norm, w_up, dw_w, dw_b, w_down, fin, *, tt, fc=1024, final_norm=False):
    d_ff = w_down[0].shape[1]
    rows = tt * SUBLANES
    halo = (dw_w.shape[0] - 1) * SUBLANES
    assert d_ff % fc == 0 and rows >= halo
    consts = (_row(norm), dw_w, _row(dw_b), _row(fin))
    scratch = [pltpu.VMEM((halo, d_ff), _F32)]
    body = functools.partial(_ffn_kernel, fc=fc, final_norm=final_norm)
    return _time_tiled_call(body, x, consts, (w_up, w_down), scratch, tt=tt, name="conv_ffn")


def kernel(x, mem, ab_norm, ab_w_in, a_conv_w, a_conv_b, a_gate_x_w, a_gate_x_b, a_gate_a_w, a_gate_a_b, a_lambda, b_group_w, b_group_b, b_scale, ab_w_out, c_norm, c_w_pw1, c_b_pw1, c_dw_w, c_dw_b, c_ln_g, c_ln_b, c_w_pw2, c_b_pw2, xa_norm, xa_mem_norm, xa_wq, xa_wk, xa_wv, xa_wo, f_norm, f_w_up, f_dw_w, f_dw_b, f_w_down, final_norm):
    depth = xa_norm.shape[0]
    tt = 64
    for layer in range(depth):
        if layer % 2 == 0:
            i = layer // 2
            x = _mixer_ab(x, ab_norm[i], (ab_w_in, i), a_conv_w[i], a_conv_b[i], a_gate_x_w[i],
                          a_gate_x_b[i], a_gate_a_w[i], a_gate_a_b[i], a_lambda[i],
                          b_group_w[i], b_group_b[i], b_scale[i], (ab_w_out, i), tt=tt)
        else:
            j = layer // 2
            x = _conformer(x, c_norm[j], (c_w_pw1, j), c_b_pw1[j], c_dw_w[j], c_dw_b[j],
                           c_ln_g[j], c_ln_b[j], (c_w_pw2, j), c_b_pw2[j], tt=tt)
        k, v = _mem_kv(mem, xa_mem_norm[layer], (xa_wk, layer), (xa_wv, layer))
        x = _cross_attn(x, k, v, xa_norm[layer], (xa_wq, layer), (xa_wo, layer),
                        n_heads=XA_HEADS, ts=1024)
        x = _conv_ffn(x, f_norm[layer], (f_w_up, layer), f_dw_w[layer], f_dw_b[layer],
                      (f_w_down, layer), final_norm, tt=tt, final_norm=(layer == depth - 1))
    return x
```

```python
import functools
import math

import jax
import jax.numpy as jnp
from jax import lax
from jax.experimental import pallas as pl
from jax.experimental.pallas import tpu as pltpu

EPS = 1e-6
C_RG = 8.0
POOL_WINDOWS = (2, 4, 8, 16)
XA_HEADS = 4
SUBLANES = 8
LANES = 128
VMEM_LIMIT_BYTES = 56 * 1024 * 1024

_BF = jnp.bfloat16
_F32 = jnp.float32


def _dot(a, b):
    return jnp.dot(a, b, preferred_element_type=_F32)


def _rms(x, g):
    return x * lax.rsqrt(jnp.mean(x * x, axis=-1, keepdims=True) + EPS) * g


def _sigmoid(x):
    return 1.0 / (1.0 + jnp.exp(-x))


def _gelu_tanh(x):
    c = math.sqrt(2.0 / math.pi)
    return 0.5 * x * (1.0 + jnp.tanh(c * (x + 0.044715 * (x * x * x))))


def _softplus(x):
    return jnp.maximum(x, 0.0) + jnp.log1p(jnp.exp(-jnp.abs(x)))


def _load_time_major(x_ref):
    tt, b, d = x_ref.shape
    return x_ref[...].reshape(tt * b, d)


def _store_time_major(o_ref, v):
    o_ref[...] = v.reshape(o_ref.shape)


def _const_spec(shape):
    nd = len(shape)
    return pl.BlockSpec(shape, lambda *_: (0,) * nd, pipeline_mode=pl.Buffered(1))


def _params(n_grid):
    return pltpu.CompilerParams(
        dimension_semantics=("arbitrary",) * n_grid,
        vmem_limit_bytes=VMEM_LIMIT_BYTES,
    )


def _row(v):
    return v.reshape(1, -1)


STAGE_BYTES = 1 << 20
STAGE_SLOTS = 4


def _stage_rows(n_rows, n_cols):
    rc = 1 << int(math.log2(max(SUBLANES, STAGE_BYTES // (4 * n_cols))))
    while n_rows % rc:
        rc //= 2
    assert rc >= SUBLANES
    return rc


def _stage_to_bf16(w_hbm, dst, stage, sem):
    n_slots, rc, _ = stage.shape
    copies = [pltpu.make_async_copy(w_hbm.at[pl.ds(k * rc, rc), :], stage.at[k % n_slots],
                                    sem.at[k % n_slots]) for k in range(dst.shape[0] // rc)]
    ahead = n_slots - 1
    for c in copies[:ahead]:
        c.start()
    for k, c in enumerate(copies):
        if k + ahead < len(copies):
            copies[k + ahead].start()
        c.wait()
        dst[k * rc:(k + 1) * rc, :] = stage[k % n_slots].astype(_BF)


def _weight_scratch(weights):
    out = []
    for w, _ in weights:
        _, r, c = w.shape
        out += [pltpu.VMEM((r, c), _BF), pltpu.VMEM((STAGE_SLOTS, _stage_rows(r, c), c), _F32),
                pltpu.SemaphoreType.DMA((STAGE_SLOTS,))]
    return out


def _stage_weights(weights, w_hbms, w_scratch):
    for k, (_, layer) in enumerate(weights):
        dst, stage, sem = w_scratch[3 * k:3 * k + 3]
        _stage_to_bf16(w_hbms[k].at[layer], dst, stage, sem)


def _time_tiled_call(body, x, consts, weights, scratch, *, tt, name):
    batch, seq, d = x.shape
    n_steps = seq // tt
    assert batch == SUBLANES and seq % tt == 0 and n_steps >= 2
    n_c, n_w, n_s = len(consts), len(weights), len(scratch)

    def kern(*refs):
        x_hbm, c_refs = refs[0], refs[1:1 + n_c]
        w_hbms, o_hbm = refs[1 + n_c:1 + n_c + n_w], refs[1 + n_c + n_w]
        rest = refs[2 + n_c + n_w:]
        s_refs, w_scratch = rest[:n_s], rest[n_s:n_s + 3 * n_w]
        xbuf, obuf, sem_in, sem_out = rest[n_s + 3 * n_w:]
        i = pl.program_id(0)
        slot = lax.rem(i, 2)

        def copies_in(step, sl):
            return [pltpu.make_async_copy(x_hbm.at[b, pl.ds(step * tt, tt), :],
                                          xbuf.at[sl, :, b, :], sem_in.at[sl]) for b in range(batch)]

        def copies_out(step, sl):
            return [pltpu.make_async_copy(obuf.at[sl, :, b, :],
                                          o_hbm.at[b, pl.ds(step * tt, tt), :], sem_out.at[sl])
                    for b in range(batch)]

        @pl.when(i == 0)
        def _():
            for c in copies_in(0, 0):
                c.start()
            _stage_weights(weights, w_hbms, w_scratch)

        @pl.when(i + 1 < n_steps)
        def _():
            for c in copies_in(i + 1, 1 - slot):
                c.start()

        for c in copies_in(i, slot):
            c.wait()

        @pl.when(i >= 2)
        def _():
            for c in copies_out(i - 2, slot):
                c.wait()

        body(xbuf.at[slot], *c_refs, *w_scratch[0::3], obuf.at[slot], *s_refs)

        for c in copies_out(i, slot):
            c.start()

        @pl.when(i == n_steps - 1)
        def _():
            for c in copies_out(i - 1, 1 - slot) + copies_out(i, slot):
                c.wait()

    any_spec = pl.BlockSpec(memory_space=pl.ANY)
    return pl.pallas_call(
        kern,
        grid=(n_steps,),
        in_specs=[any_spec] + [_const_spec(a.shape) for a in consts] + [any_spec] * n_w,
        out_specs=any_spec,
        out_shape=jax.ShapeDtypeStruct(x.shape, _F32),
        scratch_shapes=list(scratch) + _weight_scratch(weights) + [
            pltpu.VMEM((2, tt, batch, d), _F32),
            pltpu.VMEM((2, tt, batch, d), _F32),
            pltpu.SemaphoreType.DMA((2,)),
            pltpu.SemaphoreType.DMA((2,)),
        ],
        compiler_params=_params(1),
        name=name,
    )(x, *consts, *[w for w, _ in weights])


def _ab_kernel(x_ref, g_ref, cw_ref, cb_ref, wg_ref, bgx_ref, bga_ref, lam_ref,
               wp_ref, bp_ref, sc_ref, win_ref, wout_ref, o_ref,
               zrec_buf, pool_buf, a_buf, bx_buf, h_buf, h_carry, *, wa, wb, n_heads, cb_w):
    i = pl.program_id(0)
    tt = x_ref.shape[0]
    rows = tt * SUBLANES
    conv_k = cw_ref.shape[0]
    halo_c = (conv_k - 1) * SUBLANES
    halo_p = POOL_WINDOWS[-1] * SUBLANES
    hd = wa // n_heads
    hdb = wb // len(POOL_WINDOWS)

    @pl.when(i == 0)
    def _():
        zrec_buf[0:halo_c, :] = jnp.zeros((halo_c, wa), _F32)
        pool_buf[0:halo_p, :] = jnp.zeros((halo_p, wb), _F32)
        h_carry[...] = jnp.zeros_like(h_carry)

    x = _load_time_major(x_ref)
    xn = _rms(x, g_ref[...]).astype(_BF)
    sp = _softplus(-lam_ref[...])

    def in_proj(p):
        return (_dot(xn, win_ref[:, wa + p * cb_w:wa + (p + 1) * cb_w]),
                _dot(xn, win_ref[:, p * cb_w:(p + 1) * cb_w]))

    def pooled(z_pool):
        pool_buf[halo_p:halo_p + rows, :] = z_pool
        t_idx = i * tt + lax.broadcasted_iota(jnp.int32, (rows, 1), 0) // SUBLANES
        yb = []
        for g, w in enumerate(POOL_WINDOWS):
            gs = slice(g * hdb, (g + 1) * hdb)
            cur = pool_buf[:, gs]
            shift = 1
            while shift < w:
                cur = cur[shift * SUBLANES:, :] + cur[:cur.shape[0] - shift * SUBLANES, :]
                shift *= 2
            win_sum = cur[cur.shape[0] - rows:, :]
            cnt = jnp.minimum(t_idx + 1, w).astype(_F32)
            pm = (win_sum / cnt - z_pool[:, gs]).astype(_BF)
            yb.append((_dot(pm, wp_ref[g]) + bp_ref[:, gs]) * sc_ref[:, gs])
        pool_buf[0:halo_p, :] = pool_buf[rows:rows + halo_p, :]
        return jnp.concatenate(yb, axis=1).astype(_BF)

    out = x
    nxt = in_proj(0)
    n_blocks = wa // cb_w
    for p in range(n_blocks):
        cs = slice(p * cb_w, (p + 1) * cb_w)
        z_rec, z_gate = nxt

        zrec_buf[halo_c:halo_c + rows, cs] = z_rec
        xr = cb_ref[:, cs] + cw_ref[0:1, cs] * zrec_buf[0:rows, cs]
        for k in range(1, conv_k):
            xr = xr + cw_ref[k:k + 1, cs] * zrec_buf[k * SUBLANES:k * SUBLANES + rows, cs]
        zrec_buf[0:halo_c, cs] = zrec_buf[rows:rows + halo_c, cs]

        xr_b = xr.astype(_BF)
        heads = range(p * (cb_w // hd), (p + 1) * (cb_w // hd))
        gates = [_dot(xr_b[:, (h_idx * hd) % cb_w:(h_idx * hd) % cb_w + hd], wg_ref[h_idx])
                 for h_idx in heads]
        if p + 1 < n_blocks:
            nxt = in_proj(p + 1)
        for h_idx, gg in zip(heads, gates):
            hs = slice(h_idx * hd, (h_idx + 1) * hd)
            ls = slice((h_idx * hd) % cb_w, (h_idx * hd) % cb_w + hd)
            gate_x = _sigmoid(gg[:, :hd] + bgx_ref[:, hs])
            gate_a = _sigmoid(gg[:, hd:] + bga_ref[:, hs])
            a = jnp.exp((-C_RG) * gate_a * sp[:, hs])
            a_buf[:, hs] = a
            bx_buf[:, hs] = jnp.sqrt(1.0 - a * a) * (gate_x * xr[:, ls])

        h = h_carry[:, cs]
        for t in range(tt):
            ts = slice(t * SUBLANES, (t + 1) * SUBLANES)
            h = a_buf[ts, cs] * h + bx_buf[ts, cs]
            h_buf[ts, cs] = h
        h_carry[:, cs] = h

        y_a = (_gelu_tanh(z_gate) * h_buf[:, cs]).astype(_BF)
        out = out + _dot(y_a, wout_ref[cs, :])
        if p == 0:
            z_pool = _dot(xn, win_ref[:, 2 * wa:2 * wa + wb])
        elif p == 1:
            y_b = pooled(z_pool)
        elif p == 2:
            out = out + _dot(y_b, wout_ref[wa:wa + wb, :])

    _store_time_major(o_ref, out)


def _mixer_ab(x, norm, w_in, conv_w, conv_b, w_gx, b_gx, w_ga, b_ga, lam, w_pool, b_pool,
              pool_scale, w_out, *, tt):
    d = x.shape[2]
    n_heads, hd, _ = w_gx.shape
    wa = n_heads * hd
    wb = w_pool.shape[0] * w_pool.shape[1]
    rows = tt * SUBLANES
    halo_c = (conv_w.shape[0] - 1) * SUBLANES
    halo_p = POOL_WINDOWS[-1] * SUBLANES
    assert rows >= halo_p and w_in[0].shape[1:] == (d, 2 * wa + wb)
    wg = jnp.concatenate([w_gx, w_ga], axis=-1).astype(_BF)
    consts = (_row(norm), conv_w, _row(conv_b), wg, _row(b_gx), _row(b_ga),
              _row(lam), w_pool.astype(_BF), _row(b_pool), _row(pool_scale))
    scratch = [
        pltpu.VMEM((halo_c + rows, wa), _F32),
        pltpu.VMEM((halo_p + rows, wb), _F32),
        pltpu.VMEM((rows, wa), _F32),
        pltpu.VMEM((rows, wa), _F32),
        pltpu.VMEM((rows, wa), _F32),
        pltpu.VMEM((SUBLANES, wa), _F32),
    ]
    body = functools.partial(_ab_kernel, wa=wa, wb=wb, n_heads=n_heads, cb_w=2 * hd)
    return _time_tiled_call(body, x, consts, (w_in, w_out), scratch, tt=tt, name="mixer_ab")


CONV_CHUNKS_PER_TRIP = 2


def _conf_kernel(x_ref, g_ref, b1_ref, dw_ref, db_ref, w1_ref, o_ref, glu_buf, *, chunk):
    i = pl.program_id(0)
    tt, d = x_ref.shape[0], x_ref.shape[2]
    rows = tt * SUBLANES
    conv_k = dw_ref.shape[0]
    halo = (conv_k - 1) * SUBLANES

    @pl.when(i == 0)
    def _():
        glu_buf[0:halo, 0:d] = jnp.zeros((halo, d), _F32)

    x = _load_time_major(x_ref)
    hr = rows // 2
    for r0 in (0, hr):
        xn = _rms(x[r0:r0 + hr], g_ref[...]).astype(_BF)
        ha = _dot(xn, w1_ref[:, 0:d]) + b1_ref[:, 0:d]
        hb = _dot(xn, w1_ref[:, d:2 * d]) + b1_ref[:, d:2 * d]
        glu_buf[halo + r0:halo + r0 + hr, 0:d] = ha * _sigmoid(hb)

    def conv_trip(c, carry):
        for s in range(CONV_CHUNKS_PER_TRIP):
            r0 = pl.multiple_of((c * CONV_CHUNKS_PER_TRIP + s) * chunk, chunk)
            for j in range(d // LANES):
                ls = slice(j * LANES, (j + 1) * LANES)
                acc = db_ref[:, ls] + dw_ref[0:1, ls] * glu_buf[pl.ds(r0, chunk), ls]
                for k in range(1, conv_k):
                    acc = acc + dw_ref[k:k + 1, ls] * glu_buf[pl.ds(r0 + k * SUBLANES, chunk), ls]
                t0 = pl.multiple_of(r0 // SUBLANES, chunk // SUBLANES)
                o_ref[pl.ds(t0, chunk // SUBLANES), :, ls] = acc.reshape(
                    chunk // SUBLANES, SUBLANES, LANES)
        return carry

    lax.fori_loop(0, rows // (chunk * CONV_CHUNKS_PER_TRIP), conv_trip, 0)
    glu_buf[0:halo, 0:d] = glu_buf[rows:rows + halo, 0:d]


def _conformer_conv(x, norm, w1, b1, dw_w, dw_b, *, tt, chunk=64):
    d = x.shape[2]
    rows = tt * SUBLANES
    halo = (dw_w.shape[0] - 1) * SUBLANES
    assert rows >= halo and rows % (chunk * CONV_CHUNKS_PER_TRIP) == 0
    consts = (_row(norm), _row(b1), dw_w, _row(dw_b))
    scratch = [pltpu.VMEM((halo + rows, d + LANES), _F32)]
    body = functools.partial(_conf_kernel, chunk=chunk)
    return _time_tiled_call(body, x, consts, (w1,), scratch, tt=tt, name="conformer")


def _kv_kernel(m_ref, g_ref, wk_hbm, wv_hbm, k_ref, v_ref, *w_scratch, weights):
    @pl.when(pl.program_id(0) == 0)
    def _():
        _stage_weights(weights, (wk_hbm, wv_hbm), w_scratch)

    wk_ref, wv_ref = w_scratch[0::3]
    m = _rms(m_ref[0], g_ref[...]).astype(_BF)
    k_ref[0] = _dot(m, wk_ref[...]).astype(_BF)
    v_ref[0] = _dot(m, wv_ref[...]).astype(_BF)


def _mem_kv(mem, mem_norm, wk, wv):
    b, m, d = mem.shape
    blk = pl.BlockSpec((1, m, d), lambda i: (i, 0, 0))
    any_spec = pl.BlockSpec(memory_space=pl.ANY)
    return pl.pallas_call(
        functools.partial(_kv_kernel, weights=(wk, wv)),
        grid=(b,),
        in_specs=[blk, _const_spec((1, d)), any_spec, any_spec],
        out_specs=[blk, blk],
        out_shape=[jax.ShapeDtypeStruct((b, m, d), _BF)] * 2,
        scratch_shapes=_weight_scratch((wk, wv)),
        compiler_params=_params(1),
        name="mem_kv",
    )(mem, _row(mem_norm), wk[0], wv[0])


def _xa_kernel(*refs, n_heads, weights, conv_tail):
    n_in = 8 if conv_tail else 4
    n_w = len(weights)
    x_ref, g_ref, k_ref, v_ref = refs[0], refs[n_in - 3], refs[n_in - 2], refs[n_in - 1]
    w_hbms, o_ref, w_scratch = refs[n_in:n_in + n_w], refs[n_in + n_w], refs[n_in + n_w + 1:]

    @pl.when((pl.program_id(0) == 0) & (pl.program_id(1) == 0))
    def _():
        _stage_weights(weights, w_hbms, w_scratch)

    wq_ref, wo_ref = w_scratch[0], w_scratch[3]
    d = x_ref.shape[2]
    hd = d // n_heads
    x = x_ref[0]
    if conv_tail:
        c_ref, lg_ref, lb_ref, b2_ref = refs[1:5]
        c = c_ref[0]
        cc = c - jnp.mean(c, axis=-1, keepdims=True)
        y = cc * lax.rsqrt(jnp.mean(cc * cc, axis=-1, keepdims=True) + EPS) * lg_ref[...] + lb_ref[...]
        x = x + _dot((y * _sigmoid(y)).astype(_BF), w_scratch[6][...]) + b2_ref[...]
    hr = x.shape[0] // 2
    q = jnp.concatenate(
        [(_dot(_rms(x[r0:r0 + hr], g_ref[...]).astype(_BF), wq_ref[...]) * (hd ** -0.5)).astype(_BF)
         for r0 in (0, hr)], axis=0)
    outs = []
    for h in range(n_heads):
        hs = slice(h * hd, (h + 1) * hd)
        s = lax.dot_general(q[:, hs], k_ref[0, :, hs], (((1,), (1,)), ((), ())),
                            preferred_element_type=_F32)
        e = jnp.exp(s - jnp.max(s, axis=-1, keepdims=True))
        inv = 1.0 / jnp.sum(e, axis=-1, keepdims=True)
        outs.append(_dot(e.astype(_BF), v_ref[0, :, hs]) * inv)
    o = jnp.concatenate(outs, axis=1).astype(_BF)
    o_ref[0] = x + _dot(o, wo_ref[...])


def _cross_attn(x, k, v, norm, wq, wo, *, n_heads, ts, conv_tail=None):
    batch, seq, d = x.shape
    hd = d // n_heads
    assert seq % ts == 0 and math.log2(hd) % 2 == 0
    m = k.shape[1]
    tile = pl.BlockSpec((1, ts, d), lambda b, i: (b, i, 0))
    kv = pl.BlockSpec((1, m, d), lambda b, i: (b, 0, 0))
    any_spec = pl.BlockSpec(memory_space=pl.ANY)
    row_spec = _const_spec((1, d))
    weights, tail_specs, tail_args = (wq, wo), [], []
    if conv_tail is not None:
        conv, ln_g, ln_b, w2, b2 = conv_tail
        weights = (wq, wo, w2)
        tail_specs = [tile, row_spec, row_spec, row_spec]
        tail_args = [conv, _row(ln_g), _row(ln_b), _row(b2)]
    return pl.pallas_call(
        functools.partial(_xa_kernel, n_heads=n_heads, weights=weights,
                          conv_tail=conv_tail is not None),
        grid=(batch, seq // ts),
        in_specs=[tile] + tail_specs + [row_spec, kv, kv] + [any_spec] * len(weights),
        out_specs=tile,
        out_shape=jax.ShapeDtypeStruct(x.shape, _F32),
        scratch_shapes=_weight_scratch(weights),
        compiler_params=_params(2),
        name="cross_attn",
    )(x, *tail_args, _row(norm), k, v, *[w for w, _ in weights])


def _ffn_kernel(x_ref, g_ref, dw_ref, db_ref, fin_ref, wup_ref, wdn_ref, o_ref, g_halo,
                *, fc, final_norm):
    i = pl.program_id(0)
    tt = x_ref.shape[0]
    rows = tt * SUBLANES
    d_ff = wdn_ref.shape[0]
    conv_k = dw_ref.shape[0]
    halo = (conv_k - 1) * SUBLANES

    @pl.when(i == 0)
    def _():
        g_halo[...] = jnp.zeros_like(g_halo)

    x = _load_time_major(x_ref)
    hr = rows // 2
    xn_a = _rms(x[:hr], g_ref[...]).astype(_BF)
    xn_b = _rms(x[hr:], g_ref[...]).astype(_BF)

    def up(c):
        wg = wup_ref[:, c * fc:(c + 1) * fc]
        wu = wup_ref[:, d_ff + c * fc:d_ff + (c + 1) * fc]
        return (jnp.concatenate([_dot(xn_a, wg), _dot(xn_b, wg)], axis=0),
                jnp.concatenate([_dot(xn_a, wu), _dot(xn_b, wu)], axis=0))

    acc = x
    n_chunks = d_ff // fc
    nxt = up(0)
    for c in range(n_chunks):
        cs = slice(c * fc, (c + 1) * fc)
        g, u = nxt
        if c + 1 < n_chunks:
            nxt = up(c + 1)
        g_ext = jnp.concatenate([g_halo[:, cs], g], axis=0)
        g_halo[:, cs] = g[rows - halo:, :]
        conv = db_ref[:, cs] + dw_ref[0:1, cs] * g_ext[0:rows, :]
        for k in range(1, conv_k):
            conv = conv + dw_ref[k:k + 1, cs] * g_ext[k * SUBLANES:k * SUBLANES + rows, :]
        act = (_gelu_tanh(conv) * u).astype(_BF)
        acc = acc + _dot(act, wdn_ref[cs, :])
    if final_norm:
        acc = _rms(acc, fin_ref[...])
    _store_time_major(o_ref, acc)


def _conv_ffn(x, norm, w_up, dw_w, dw_b, w_down, fin, *, tt, fc=1024, final_norm=False):
    d_ff = w_down[0].shape[1]
    rows = tt * SUBLANES
    halo = (dw_w.shape[0] - 1) * SUBLANES
    assert d_ff % fc == 0 and rows >= halo
    consts = (_row(norm), dw_w, _row(dw_b), _row(fin))
    scratch = [pltpu.VMEM((halo, d_ff), _F32)]
    body = functools.partial(_ffn_kernel, fc=fc, final_norm=final_norm)
    return _time_tiled_call(body, x, consts, (w_up, w_down), scratch, tt=tt, name="conv_ffn")


def kernel(x, mem, ab_norm, ab_w_in, a_conv_w, a_conv_b, a_gate_x_w, a_gate_x_b, a_gate_a_w, a_gate_a_b, a_lambda, b_group_w, b_group_b, b_scale, ab_w_out, c_norm, c_w_pw1, c_b_pw1, c_dw_w, c_dw_b, c_ln_g, c_ln_b, c_w_pw2, c_b_pw2, xa_norm, xa_mem_norm, xa_wq, xa_wk, xa_wv, xa_wo, f_norm, f_w_up, f_dw_w, f_dw_b, f_w_down, final_norm):
    depth = xa_norm.shape[0]
    tt = 64
    for layer in range(depth):
        conv_tail = None
        if layer % 2 == 0:
            i = layer // 2
            x = _mixer_ab(x, ab_norm[i], (ab_w_in, i), a_conv_w[i], a_conv_b[i], a_gate_x_w[i],
                          a_gate_x_b[i], a_gate_a_w[i], a_gate_a_b[i], a_lambda[i],
                          b_group_w[i], b_group_b[i], b_scale[i], (ab_w_out, i), tt=tt)
        else:
            j = layer // 2
            conv = _conformer_conv(x, c_norm[j], (c_w_pw1, j), c_b_pw1[j], c_dw_w[j], c_dw_b[j], tt=tt)
            conv_tail = (conv, c_ln_g[j], c_ln_b[j], (c_w_pw2, j), c_b_pw2[j])
        k, v = _mem_kv(mem, xa_mem_norm[layer], (xa_wk, layer), (xa_wv, layer))
        x = _cross_attn(x, k, v, xa_norm[layer], (xa_wq, layer), (xa_wo, layer),
                        n_heads=XA_HEADS, ts=1024, conv_tail=conv_tail)
        x = _conv_ffn(x, f_norm[layer], (f_w_up, layer), f_dw_w[layer], f_dw_b[layer],
                      (f_w_down, layer), final_norm, tt=tt, final_norm=(layer == depth - 1))
    return x
```
